```python
import math
import jax
import jax.numpy as jnp
from jax import lax
import numpy as np

D_MODEL = 4096
BATCH = 1
SEQ = 16384
DEPTH = 4

CTX_LEN = 256
GRID_W = 64
Q_BLOCK = 128
ROPE_THETA = 10000.0
NORM_EPS = 1e-6

DIFF_HEADS = D_MODEL // 512
DIFF_DH = 64
DIFF_VD = 2 * DIFF_DH
DIFF_QK_COLS = 2 * DIFF_HEADS * DIFF_DH
DIFF_WIDTH = DIFF_HEADS * DIFF_VD

MLA_HEADS = D_MODEL // 256
MLA_NOPE = 128
MLA_ROPE = 64
MLA_QK = MLA_NOPE + MLA_ROPE
MLA_V = 128
MLA_Q_RANK = D_MODEL // 4
MLA_KV_RANK = 512
MLA_WIDTH = MLA_HEADS * MLA_V

NA_HEADS = D_MODEL // 512
NA_DH = 128
NA_KH = 8
NA_KW = 16
NA_WIDTH = NA_HEADS * NA_DH

D_FF = 4 * D_MODEL
ADA_RANK = 256
GATE_RANK = 512
N_BRANCH = 3
N_MOD = 6

IN_WIDTHS = (DIFF_QK_COLS, DIFF_QK_COLS, DIFF_WIDTH,
             MLA_Q_RANK, MLA_KV_RANK, MLA_ROPE,
             NA_WIDTH, NA_WIDTH, NA_WIDTH)
IN_COLS = sum(IN_WIDTHS)

kernel_name = 'hybrid_gated_diffattn_mla_natten_dit'


def rms_norm(x, g):
    xf = x.astype(jnp.float32)
    y = xf * lax.rsqrt(jnp.mean(jnp.square(xf), axis=-1, keepdims=True) + NORM_EPS)
    return (y * g.astype(jnp.float32)).astype(x.dtype)


def modulate(x, g, shift, scale):
    return rms_norm(x, g) * (1 + scale) + shift


def ada_mod(cond, w_down, w_up, b):
    return jnp.split((jax.nn.silu(cond) @ w_down) @ w_up + b, N_MOD, axis=-1)


def split_heads(t, n_heads):
    return t.reshape(*t.shape[:-1], n_heads, t.shape[-1] // n_heads)


def query_blocks(q):
    b, s = q.shape[:2]
    return q.reshape(b, s // Q_BLOCK, Q_BLOCK, *q.shape[2:]).swapaxes(0, 1)


def merge_blocks(o):
    return o.swapaxes(0, 1).reshape(o.shape[1], -1, *o.shape[3:])


def axial_rope_angles(n_tokens, dim):
    t = jnp.arange(n_tokens)
    row = (t // GRID_W).astype(jnp.float32)
    col = (t % GRID_W).astype(jnp.float32)
    axis_dim = dim // 2
    inv = ROPE_THETA ** (-jnp.arange(0, axis_dim, 2, dtype=jnp.float32) / axis_dim)
    ang = jnp.stack([row[:, None] * inv, col[:, None] * inv], axis=0)
    return jnp.cos(ang), jnp.sin(ang)


def apply_axial_rope(x, cos, sin):
    *lead, dim = x.shape
    xf = x.astype(jnp.float32).reshape(*lead, 2, 2, dim // 4)
    cs = jnp.swapaxes(cos, 0, 1)[None, :, None]
    sn = jnp.swapaxes(sin, 0, 1)[None, :, None]
    x1, x2 = xf[..., 0, :], xf[..., 1, :]
    out = jnp.stack([x1 * cs - x2 * sn, x2 * cs + x1 * sn], axis=-2)
    return out.reshape(*lead, dim).astype(x.dtype)


def softmax_attend(q, k, v, scale):
    s = jnp.einsum('bqhd,bkhd->bhqk', q, k, preferred_element_type=jnp.float32) * scale
    p = jax.nn.softmax(s, axis=-1).astype(v.dtype)
    return jnp.einsum('bhqk,bkhd->bqhd', p, v)


def diff_attention(q_l, k_l, v_l, q_c, k_c, v_c, lam_p, q_g, k_g, sub_g, lam_init, rope, need_ctx):
    b = q_l.shape[0]
    scale = DIFF_DH ** -0.5
    lp = lam_p.astype(jnp.float32)
    lam = jnp.exp(jnp.sum(lp[0] * lp[1])) - jnp.exp(jnp.sum(lp[2] * lp[3])) + lam_init

    def prep(t, g, rotate):
        t = rms_norm(split_heads(t, 2 * DIFF_HEADS), g)
        if rotate:
            t = apply_axial_rope(t, *rope)
        return t.reshape(b, t.shape[1], 2, DIFF_HEADS, DIFF_DH)

    def attend(q, k, v):
        s = jnp.einsum('bqmhd,bkmhd->bmhqk', q, k, preferred_element_type=jnp.float32) * scale
        p = jax.nn.softmax(s, axis=-1)
        w = (p[:, 0] - lam * p[:, 1]).astype(v.dtype)
        o = jnp.einsum('bhqk,bkhd->bqhd', w, v)
        return (rms_norm(o, sub_g) * (1.0 - lam_init)).reshape(b, q.shape[1], DIFF_WIDTH)

    k_ctx = prep(k_c, k_g, False)
    v_ctx = split_heads(v_c, DIFF_HEADS)
    k_all = jnp.concatenate([k_ctx, prep(k_l, k_g, True)], axis=1)
    v_all = jnp.concatenate([v_ctx, split_heads(v_l, DIFF_HEADS)], axis=1)
    o_lat = merge_blocks(lax.map(lambda qb: attend(qb, k_all, v_all), query_blocks(prep(q_l, q_g, True))))
    o_ctx = attend(prep(q_c, q_g, False), k_ctx, v_ctx) if need_ctx else None
    return o_lat, o_ctx


def mla_attention(cq_l, ckv_l, kpe_l, cq_c, ckv_c, kpe_c, q_a_g, kv_a_g, w_uq, w_ukv, q_g, k_g, rope, need_ctx):
    b = cq_l.shape[0]
    scale = MLA_QK ** -0.5

    def rope_tail(t):
        return jnp.concatenate([t[..., :MLA_NOPE], apply_axial_rope(t[..., MLA_NOPE:], *rope)], axis=-1)

    def queries(cq, rotate):
        q = rms_norm(split_heads(rms_norm(cq, q_a_g) @ w_uq, MLA_HEADS), q_g)
        return rope_tail(q) if rotate else q

    def keys_values(ckv, kpe, rotate):
        kv = split_heads(rms_norm(ckv, kv_a_g) @ w_ukv, MLA_HEADS)
        k_nope, v = kv[..., :MLA_NOPE], kv[..., MLA_NOPE:]
        k_pe = jnp.broadcast_to(kpe[:, :, None, :], (*k_nope.shape[:3], MLA_ROPE))
        k = rms_norm(jnp.concatenate([k_nope, k_pe], axis=-1), k_g)
        return (rope_tail(k) if rotate else k), v

    k_ctx, v_ctx = keys_values(ckv_c, kpe_c, False)
    k_lat, v_lat = keys_values(ckv_l, kpe_l, True)
    k_all = jnp.concatenate([k_ctx, k_lat], axis=1)
    v_all = jnp.concatenate([v_ctx, v_lat], axis=1)
    o_lat = merge_blocks(lax.map(lambda qb: softmax_attend(qb, k_all, v_all, scale),
                                 query_blocks(queries(cq_l, True))))
    o_lat = o_lat.reshape(b, -1, MLA_WIDTH)
    o_ctx = softmax_attend(queries(cq_c, False), k_ctx, v_ctx, scale).reshape(b, -1, MLA_WIDTH) if need_ctx else None
    return o_lat, o_ctx


def na_attention(q_l, k_l, v_l, q_c, k_c, v_c, q_g, k_g, rpb, need_ctx):
    b, s, _ = q_l.shape
    rows = s // GRID_W
    kh = min(NA_KH, rows)
    n_win = kh * NA_KW
    scale = NA_DH ** -0.5
    grid = (b, rows, GRID_W, NA_HEADS, NA_DH)
    q_lat = rms_norm(split_heads(q_l, NA_HEADS), q_g).reshape(grid)
    k_lat = rms_norm(split_heads(k_l, NA_HEADS), k_g).reshape(grid)
    v_lat = split_heads(v_l, NA_HEADS).reshape(grid)
    k_ctx = rms_norm(split_heads(k_c, NA_HEADS), k_g)
    v_ctx = split_heads(v_c, NA_HEADS)
    col = jnp.arange(GRID_W)
    col_idx = jnp.clip(col - NA_KW // 2, 0, GRID_W - NA_KW)[:, None] + jnp.arange(NA_KW)[None, :]
    dc = col_idx - col[:, None] + (NA_KW - 1)
    rpb_f = rpb.astype(jnp.float32)

    def one_row(r):
        r0 = jnp.clip(r - kh // 2, 0, rows - kh)
        k_win = lax.dynamic_slice_in_dim(k_lat, r0, kh, axis=1)[:, :, col_idx]
        v_win = lax.dynamic_slice_in_dim(v_lat, r0, kh, axis=1)[:, :, col_idx]
        q_row = lax.dynamic_index_in_dim(q_lat, r, axis=1, keepdims=False)
        dr = r0 + jnp.arange(kh) - r + (NA_KH - 1)
        bias = rpb_f[:, dr[None, :, None], dc[:, None, :]]
        s_win = jnp.einsum('bqhd,brqwhd->bhqrw', q_row, k_win, preferred_element_type=jnp.float32) * scale + bias
        s_ctx = jnp.einsum('bqhd,bchd->bhqc', q_row, k_ctx, preferred_element_type=jnp.float32) * scale
        p = jax.nn.softmax(jnp.concatenate([s_win.reshape(b, NA_HEADS, GRID_W, n_win), s_ctx], axis=-1), axis=-1)
        p = p.astype(v_lat.dtype)
        p_win = p[..., :n_win].reshape(b, NA_HEADS, GRID_W, kh, NA_KW)
        return (jnp.einsum('bhqrw,brqwhd->bqhd', p_win, v_win)
                + jnp.einsum('bhqc,bchd->bqhd', p[..., n_win:], v_ctx))

    o = lax.map(one_row, jnp.arange(rows))
    o_lat = o.swapaxes(0, 1).reshape(b, s, NA_WIDTH)
    o_ctx = None
    if need_ctx:
        q_ctx = rms_norm(split_heads(q_c, NA_HEADS), q_g)
        o_ctx = softmax_attend(q_ctx, k_ctx, v_ctx, scale).reshape(b, -1, NA_WIDTH)
    return o_lat, o_ctx


def branch_merge(h, o_a, o_b, o_c, w_pa, w_pb, w_pc, w_gd, w_gu, b_g, w_o):
    g_a, g_b, g_c = jnp.split(jax.nn.sigmoid((h @ w_gd) @ w_gu + b_g), N_BRANCH, axis=-1)
    return (g_a * (o_a @ w_pa) + g_b * (o_b @ w_pb) + g_c * (o_c @ w_pc)) @ w_o


def sq_relu_mlp(h, w_up, w_down):
    return jnp.square(jax.nn.relu(h @ w_up)) @ w_down


def setup_inputs(seed: int = 0) -> dict:
    key = jax.random.key(seed)
    ks = iter(jax.random.split(key, 40))
    L = DEPTH

    def nrm(shape, scale):
        return jax.random.normal(next(ks), shape, jnp.float32) * scale

    def gain(shape):
        return 1.0 + nrm(shape, 0.02)

    return {
        'x': nrm((BATCH, SEQ, D_MODEL), 1.0),
        'c': nrm((BATCH, D_MODEL), 1.0),
        'ctx': nrm((BATCH, CTX_LEN, D_MODEL), 1.0),
        'c_ctx': nrm((D_MODEL,), 1.0),
        'w_ada_down': nrm((D_MODEL, ADA_RANK), D_MODEL ** -0.5),
        'w_ada_up': nrm((L, ADA_RANK, N_MOD * D_MODEL), 0.5 * ADA_RANK ** -0.5),
        'b_ada': nrm((L, N_MOD * D_MODEL), 0.02),
        'norm_attn_g': gain((L, D_MODEL)),
        'norm_mlp_g': gain((L, D_MODEL)),
        'w_in': nrm((L, D_MODEL, IN_COLS), D_MODEL ** -0.5),
        'diff_lambda': nrm((L, 4, DIFF_DH), 0.1),
        'diff_q_norm': gain((L, DIFF_DH)),
        'diff_k_norm': gain((L, DIFF_DH)),
        'diff_subln': gain((L, DIFF_VD)),
        'mla_q_a_norm': gain((L, MLA_Q_RANK)),
        'mla_kv_a_norm': gain((L, MLA_KV_RANK)),
        'w_mla_uq': nrm((L, MLA_Q_RANK, MLA_HEADS * MLA_QK), MLA_Q_RANK ** -0.5),
        'w_mla_ukv': nrm((L, MLA_KV_RANK, MLA_HEADS * (MLA_NOPE + MLA_V)), MLA_KV_RANK ** -0.5),
        'mla_q_norm': gain((L, MLA_QK)),
        'mla_k_norm': gain((L, MLA_QK)),
        'na_q_norm': gain((L, NA_DH)),
        'na_k_norm': gain((L, NA_DH)),
        'na_rpb': nrm((L, NA_HEADS, 2 * NA_KH - 1, 2 * NA_KW - 1), 0.1),
        'w_proj_a': nrm((L, DIFF_WIDTH, D_MODEL), DIFF_WIDTH ** -0.5),
        'w_proj_b': nrm((L, MLA_WIDTH, D_MODEL), MLA_WIDTH ** -0.5),
        'w_proj_c': nrm((L, NA_WIDTH, D_MODEL), NA_WIDTH ** -0.5),
        'w_gate_down': nrm((L, D_MODEL, GATE_RANK), D_MODEL ** -0.5),
        'w_gate_up': nrm((L, GATE_RANK, N_BRANCH * D_MODEL), GATE_RANK ** -0.5),
        'b_gate': nrm((L, N_BRANCH * D_MODEL), 0.02),
        'w_out': nrm((L, D_MODEL, D_MODEL), D_MODEL ** -0.5),
        'w_mlp_up': nrm((L, D_MODEL, D_FF), D_MODEL ** -0.5),
        'w_mlp_down': nrm((L, D_FF, D_MODEL), D_FF ** -0.5),
    }


def reference(x, c, ctx, c_ctx, w_ada_down, w_ada_up, b_ada, norm_attn_g, norm_mlp_g, w_in,
              diff_lambda, diff_q_norm, diff_k_norm, diff_subln,
              mla_q_a_norm, mla_kv_a_norm, w_mla_uq, w_mla_ukv, mla_q_norm, mla_k_norm,
              na_q_norm, na_k_norm, na_rpb,
              w_proj_a, w_proj_b, w_proj_c, w_gate_down, w_gate_up, b_gate, w_out,
              w_mlp_up, w_mlp_down):
    s = x.shape[1]
    rope_diff = axial_rope_angles(s, DIFF_DH)
    rope_mla = axial_rope_angles(s, MLA_ROPE)
    split_points = np.cumsum(IN_WIDTHS)[:-1].tolist()
    xc = ctx
    for l in range(DEPTH):
        need_ctx = l < DEPTH - 1
        lam_init = 0.8 - 0.6 * math.exp(-0.3 * l)
        sh_a, sc_a, gt_a, sh_m, sc_m, gt_m = ada_mod(c[:, None, :], w_ada_down, w_ada_up[l], b_ada[l])
        csh_a, csc_a, cgt_a, csh_m, csc_m, cgt_m = ada_mod(c_ctx, w_ada_down, w_ada_up[l], b_ada[l])
        h = modulate(x, norm_attn_g[l], sh_a, sc_a)
        hc = modulate(xc, norm_attn_g[l], csh_a, csc_a)
        pl = jnp.split(h @ w_in[l], split_points, axis=-1)
        pc = jnp.split(hc @ w_in[l], split_points, axis=-1)
        a_l, a_c = diff_attention(pl[0], pl[1], pl[2], pc[0], pc[1], pc[2], diff_lambda[l],
                                  diff_q_norm[l], diff_k_norm[l], diff_subln[l], lam_init, rope_diff, need_ctx)
        b_l, b_c = mla_attention(pl[3], pl[4], pl[5], pc[3], pc[4], pc[5], mla_q_a_norm[l], mla_kv_a_norm[l],
                                 w_mla_uq[l], w_mla_ukv[l], mla_q_norm[l], mla_k_norm[l], rope_mla, need_ctx)
        n_l, n_c = na_attention(pl[6], pl[7], pl[8], pc[6], pc[7], pc[8], na_q_norm[l], na_k_norm[l],
                                na_rpb[l], need_ctx)
        x = x + gt_a * branch_merge(h, a_l, b_l, n_l, w_proj_a[l], w_proj_b[l], w_proj_c[l],
                                    w_gate_down[l], w_gate_up[l], b_gate[l], w_out[l])
        x = x + gt_m * sq_relu_mlp(modulate(x, norm_mlp_g[l], sh_m, sc_m), w_mlp_up[l], w_mlp_down[l])
        if need_ctx:
            xc = xc + cgt_a * branch_merge(hc, a_c, b_c, n_c, w_proj_a[l], w_proj_b[l], w_proj_c[l],
                                           w_gate_down[l], w_gate_up[l], b_gate[l], w_out[l])
            xc = xc + cgt_m * sq_relu_mlp(modulate(xc, norm_mlp_g[l], csh_m, csc_m), w_mlp_up[l], w_mlp_down[l])
    return x
```

```python
import functools
import math

import numpy as np
import jax
import jax.numpy as jnp
from jax import lax
from jax.experimental import pallas as pl
from jax.experimental.pallas import tpu as pltpu

F32 = jnp.float32
BF16 = jnp.bfloat16

NORM_EPS = 1e-6
ROPE_THETA = 10000.0
GRID_W = 64
DIFF_DH = 64
DIFF_VD = 2 * DIFF_DH
MLA_NOPE = 128
MLA_ROPE = 64
MLA_QK = MLA_NOPE + MLA_ROPE
MLA_QK_PAD = 256
MLA_V = 128
NA_DH = 128
NA_KH = 8
NA_KW = 16
N_MOD = 6
N_BRANCH = 3
LOG2E = 1.4426950408889634
MASKED = -1e30
LANES = 128
VMEM_LIMIT_BYTES = 56 * 1024 * 1024


def _params(*semantics):
    return pltpu.CompilerParams(dimension_semantics=semantics, vmem_limit_bytes=VMEM_LIMIT_BYTES)


def _dot(a, b):
    return jnp.dot(a, b, preferred_element_type=F32)


def _dot_nt(a, b):
    return lax.dot_general(a, b, (((1,), (1,)), ((), ())), preferred_element_type=F32)


def _ada_kernel(cond_ref, wd_ref, wu_ref, b_ref, o_ref):
    cnd = cond_ref[...]
    act = cnd / (1.0 + jnp.exp(-cnd))
    low = _dot(act.astype(BF16), wd_ref[...])
    o_ref[0] = _dot(low.astype(BF16), wu_ref[0]) + b_ref[0]


def _ada(cond, w_down, w_up, b):
    rows, d = cond.shape
    depth, rank, n = w_up.shape
    tn = min(n, 4096)
    return pl.pallas_call(
        _ada_kernel,
        grid=(depth, n // tn),
        in_specs=[pl.BlockSpec((rows, d), lambda l, j: (0, 0)),
                  pl.BlockSpec((d, rank), lambda l, j: (0, 0)),
                  pl.BlockSpec((1, rank, tn), lambda l, j: (l, 0, j)),
                  pl.BlockSpec((1, 1, tn), lambda l, j: (l, 0, j))],
        out_specs=pl.BlockSpec((1, rows, tn), lambda l, j: (l, 0, j)),
        out_shape=jax.ShapeDtypeStruct((depth, rows, n), F32),
        compiler_params=_params("arbitrary", "arbitrary"),
        name="ada_mod",
    )(cond, w_down, w_up, b.reshape(depth, 1, n))


def _modulate_kernel(x_ref, g_ref, sh_ref, sc_ref, o_ref):
    x = x_ref[...]
    r = lax.rsqrt(jnp.mean(x * x, axis=-1, keepdims=True) + NORM_EPS)
    y = (x * r) * g_ref[...]
    o_ref[...] = (y * (1.0 + sc_ref[...]) + sh_ref[...]).astype(o_ref.dtype)


def _modulate(x, g, shift, scale, tm=256):
    m, d = x.shape
    tm = min(tm, m)
    vec = pl.BlockSpec((1, d), lambda i: (0, 0))
    return pl.pallas_call(
        _modulate_kernel,
        grid=(m // tm,),
        in_specs=[pl.BlockSpec((tm, d), lambda i: (i, 0)), vec, vec, vec],
        out_specs=pl.BlockSpec((tm, d), lambda i: (i, 0)),
        out_shape=jax.ShapeDtypeStruct((m, d), BF16),
        compiler_params=_params("arbitrary"),
        name="modulate",
    )(x, g, shift, scale)


def _mm_kernel(*refs, nk, n_norm, epilogue):
    it = iter(refs)
    a_ref, w_ref = next(it), next(it)
    g_ref = next(it) if n_norm else None
    x_ref, gate_ref = (next(it), next(it)) if epilogue == "resid" else (None, None)
    o_ref = next(it)
    acc_ref = next(it) if nk > 1 else None

    a = a_ref[...]
    if n_norm:
        af = a.astype(F32)
        if n_norm < af.shape[1]:
            normed = lax.broadcasted_iota(jnp.int32, af.shape, 1) < n_norm
            ssq = jnp.sum(jnp.where(normed, af * af, 0.0), axis=-1, keepdims=True)
            r = lax.rsqrt(ssq * (1.0 / n_norm) + NORM_EPS)
            a = jnp.where(normed, (af * r) * g_ref[...], af).astype(BF16)
        else:
            r = lax.rsqrt(jnp.mean(af * af, axis=-1, keepdims=True) + NORM_EPS)
            a = ((af * r) * g_ref[...]).astype(BF16)
    part = _dot(a, w_ref[...])

    def finish(acc):
        if epilogue == "relu2":
            acc = jnp.maximum(acc, 0.0)
            acc = acc * acc
        elif epilogue == "resid":
            acc = x_ref[...] + gate_ref[...] * acc
        o_ref[...] = acc.astype(o_ref.dtype)

    if nk == 1:
        finish(part)
    else:
        k = pl.program_id(2)

        @pl.when(k == 0)
        def _():
            acc_ref[...] = part

        @pl.when(k > 0)
        def _():
            acc_ref[...] += part

        @pl.when(k == nk - 1)
        def _():
            finish(acc_ref[...])


def _matmul(a, w, *, a_col=0, out_dtype=BF16, norm_gain=None, n_norm=0, epilogue=None,
            resid=None, gate=None, tm=1024, tn=1024, tk=None, name="matmul"):
    m = a.shape[0]
    k_dim, n = w.shape
    tm, tn, tk = min(tm, m), min(tn, n), min(tk or k_dim, k_dim)
    assert m % tm == 0 and n % tn == 0 and k_dim % tk == 0 and a_col % tk == 0, (m, n, k_dim, tm, tn, tk, a_col)
    nk = k_dim // tk
    a_off = a_col // tk
    assert not (n_norm and nk > 1)
    in_specs = [pl.BlockSpec((tm, tk), lambda i, j, k: (i, a_off + k)),
                pl.BlockSpec((tk, tn), lambda i, j, k: (k, j))]
    args = [a, w]
    if n_norm:
        in_specs.append(pl.BlockSpec((1, tk), lambda i, j, k: (0, 0)))
        args.append(norm_gain)
    aliases = {}
    if epilogue == "resid":
        aliases = {len(args): 0}
        in_specs += [pl.BlockSpec((tm, tn), lambda i, j, k: (i, j)),
                     pl.BlockSpec((1, tn), lambda i, j, k: (0, j))]
        args += [resid, gate]
    return pl.pallas_call(
        functools.partial(_mm_kernel, nk=nk, n_norm=n_norm, epilogue=epilogue),
        grid=(m // tm, n // tn, nk),
        in_specs=in_specs,
        out_specs=pl.BlockSpec((tm, tn), lambda i, j, k: (i, j)),
        out_shape=jax.ShapeDtypeStruct((m, n), out_dtype),
        scratch_shapes=[pltpu.VMEM((tm, tn), F32)] if nk > 1 else [],
        input_output_aliases=aliases,
        compiler_params=_params("arbitrary", "arbitrary", "arbitrary"),
        name=name,
    )(*args)


def _headnorm_kernel(*refs, wb, nchunk, inv_cnt, rope):
    if rope:
        x_ref, g_ref, gm_ref, gr_ref, pm_ref, cos_ref, sin_ref, o_ref = refs
    else:
        x_ref, g_ref, gm_ref, o_ref = refs
    for c in range(nchunk):
        cols = slice(c * wb, (c + 1) * wb)
        xb = x_ref[:, cols]
        xf = xb.astype(F32)
        sq = xf * xf
        hi = sq.astype(BF16)
        lo = (sq - hi.astype(F32)).astype(BF16)
        ssq = _dot(hi, gm_ref[...]) + _dot(lo, gm_ref[...])
        r = lax.rsqrt(ssq * inv_cnt + NORM_EPS)
        y = (xf * r) * g_ref[:, cols]
        if rope:
            yr = (_dot(xb, pm_ref[...]) * r) * gr_ref[:, cols]
            y = y * cos_ref[...] + yr * sin_ref[...]
        o_ref[:, cols] = y.astype(o_ref.dtype)


def _headnorm(x, col, width, *, wb, gain, gmat, inv_cnt, rope=None, tm=512, name="headnorm"):
    m = x.shape[0]
    tm = min(tm, m)
    assert m % tm == 0 and col % width == 0 and width % wb == 0
    cb = col // width
    row = pl.BlockSpec((1, width), lambda i: (0, 0))
    mat = pl.BlockSpec((wb, wb), lambda i: (0, 0))
    in_specs = [pl.BlockSpec((tm, width), lambda i: (i, cb)), row, mat]
    args = [x, gain, gmat]
    if rope is not None:
        gain_rot, pmat, cos, sin = rope
        tab = pl.BlockSpec((tm, wb), lambda i: (i, 0))
        in_specs += [row, mat, tab, tab]
        args += [gain_rot, pmat, cos, sin]
    return pl.pallas_call(
        functools.partial(_headnorm_kernel, wb=wb, nchunk=width // wb, inv_cnt=inv_cnt, rope=rope is not None),
        grid=(m // tm,),
        in_specs=in_specs,
        out_specs=pl.BlockSpec((tm, width), lambda i: (i, 0)),
        out_shape=jax.ShapeDtypeStruct((m, width), BF16),
        compiler_params=_params("arbitrary"),
        name=name,
    )(*args)


def _flash_kernel(*refs, diff, tq, nk, lam_init):
    if diff:
        q_ref, k_ref, v_ref, lp_ref, sg_ref, o_ref, qs_ref, m_ref, l_ref, acc_ref = refs
    else:
        q_ref, k_ref, v_ref, o_ref, m_ref, l_ref, acc_ref = refs
    ki = pl.program_id(2)

    @pl.when(ki == 0)
    def _():
        m_ref[...] = jnp.full(m_ref.shape, MASKED, F32)
        l_ref[...] = jnp.zeros(l_ref.shape, F32)
        acc_ref[...] = jnp.zeros(acc_ref.shape, F32)
        if diff:
            q = q_ref[...].astype(F32)
            first = lax.broadcasted_iota(jnp.int32, q.shape, 1) < DIFF_DH
            qs_ref[0:tq, :] = jnp.where(first, q, 0.0).astype(BF16)
            qs_ref[tq:2 * tq, :] = jnp.where(first, 0.0, q).astype(BF16)

    q = qs_ref[...] if diff else q_ref[...]
    s = _dot_nt(q, k_ref[...])
    m_prev = m_ref[...]
    m_new = jnp.maximum(m_prev, jnp.max(s, axis=1, keepdims=True))
    p = jnp.exp2(s - m_new[:, :1])
    alpha = jnp.exp2(m_prev - m_new)
    l_ref[...] = alpha * l_ref[...] + jnp.sum(p, axis=1, keepdims=True)
    acc_ref[...] = alpha * acc_ref[...] + _dot(p.astype(BF16), v_ref[...])
    m_ref[...] = m_new

    @pl.when(ki == nk - 1)
    def _():
        o = acc_ref[...] / l_ref[...]
        if diff:
            lp = lp_ref[...]
            lam = (jnp.exp(jnp.sum(lp[0:1] * lp[1:2], axis=1, keepdims=True))
                   - jnp.exp(jnp.sum(lp[2:3] * lp[3:4], axis=1, keepdims=True)) + lam_init)
            d = o[0:tq] - lam * o[tq:2 * tq]
            r = lax.rsqrt(jnp.mean(d * d, axis=-1, keepdims=True) + NORM_EPS)
            o = ((d * r) * sg_ref[...]) * (1.0 - lam_init)
        o_ref[...] = o.astype(o_ref.dtype)


def _flash(q, k, v, *, heads, dqk, k_col=0, v_col=0, diff=None, tq=512, tk=1280, name="flash"):
    m = q.shape[0]
    skv = k.shape[0]
    tq, tk = min(tq, m), min(tk, skv)
    assert m % tq == 0 and skv % tk == 0 and k_col % dqk == 0 and v_col % LANES == 0
    nk = skv // tk
    kb, vb = k_col // dqk, v_col // LANES
    rows = 2 * tq if diff else tq
    in_specs = [pl.BlockSpec((tq, dqk), lambda h, i, j: (i, h)),
                pl.BlockSpec((tk, dqk), lambda h, i, j: (j, kb + h)),
                pl.BlockSpec((tk, LANES), lambda h, i, j: (j, vb + h))]
    args = [q, k, v]
    scratch = []
    lam_init = 0.0
    if diff:
        lam_p, sub_g, lam_init = diff
        in_specs += [pl.BlockSpec(lam_p.shape, lambda h, i, j: (0, 0)),
                     pl.BlockSpec((1, LANES), lambda h, i, j: (0, 0))]
        args += [lam_p, sub_g]
        scratch.append(pltpu.VMEM((rows, dqk), BF16))
    scratch += [pltpu.VMEM((rows, LANES), F32), pltpu.VMEM((rows, LANES), F32), pltpu.VMEM((rows, LANES), F32)]
    return pl.pallas_call(
        functools.partial(_flash_kernel, diff=bool(diff), tq=tq, nk=nk, lam_init=lam_init),
        grid=(heads, m // tq, nk),
        in_specs=in_specs,
        out_specs=pl.BlockSpec((tq, LANES), lambda h, i, j: (i, h)),
        out_shape=jax.ShapeDtypeStruct((m, heads * LANES), BF16),
        scratch_shapes=scratch,
        compiler_params=_params("arbitrary", "arbitrary", "arbitrary"),
        name=name,
    )(*args)


def _na_kernel(q_ref, k_ref, v_ref, kc_ref, vc_ref, b_ref, o_ref, *, rb, rows):
    step = pl.program_id(1)
    kc = kc_ref[...]
    vc = vc_ref[...]
    win = NA_KH * GRID_W

    def one_row(rr, carry):
        r = step * rb + rr
        r0 = jnp.clip(r - NA_KH // 2, 0, rows - NA_KH)
        case = r0 - r + (NA_KH - 1)
        q = q_ref[pl.ds(pl.multiple_of(rr * GRID_W, GRID_W), GRID_W), :]
        start = pl.multiple_of(r0 * GRID_W, GRID_W)
        s_win = _dot_nt(q, k_ref[pl.ds(start, win), :]) + b_ref[0, case]
        s_ctx = _dot_nt(q, kc)
        m = jnp.maximum(jnp.max(s_win, axis=1, keepdims=True), jnp.max(s_ctx, axis=1, keepdims=True))
        p_win = jnp.exp2(s_win - m)
        p_ctx = jnp.exp2(s_ctx - m)
        l = jnp.sum(p_win, axis=1, keepdims=True) + jnp.sum(p_ctx, axis=1, keepdims=True)
        o = _dot(p_win.astype(BF16), v_ref[pl.ds(start, win), :]) + _dot(p_ctx.astype(BF16), vc)
        o_ref[pl.ds(pl.multiple_of(rr * GRID_W, GRID_W), GRID_W), :] = (o / l).astype(o_ref.dtype)
        return carry

    lax.fori_loop(0, rb, one_row, 0)


def _na_attention(q, k, v, v_col, kc, vc, vc_col, bias, *, heads, rb=8):
    s = q.shape[0]
    rows = s // GRID_W
    assert s % GRID_W == 0 and rows >= NA_KH and rows % rb == 0
    nctx = kc.shape[0]
    vb, vcb = v_col // NA_DH, vc_col // NA_DH
    tq = rb * GRID_W
    return pl.pallas_call(
        functools.partial(_na_kernel, rb=rb, rows=rows),
        grid=(heads, rows // rb),
        in_specs=[pl.BlockSpec((tq, NA_DH), lambda h, i: (i, h)),
                  pl.BlockSpec((s, NA_DH), lambda h, i: (0, h)),
                  pl.BlockSpec((s, NA_DH), lambda h, i: (0, vb + h)),
                  pl.BlockSpec((nctx, NA_DH), lambda h, i: (0, h)),
                  pl.BlockSpec((nctx, NA_DH), lambda h, i: (0, vcb + h)),
                  pl.BlockSpec((1,) + bias.shape[1:], lambda h, i: (h, 0, 0, 0))],
        out_specs=pl.BlockSpec((tq, NA_DH), lambda h, i: (i, h)),
        out_shape=jax.ShapeDtypeStruct((s, heads * NA_DH), BF16),
        compiler_params=_params("arbitrary", "arbitrary"),
        name="na_attention",
    )(q, k, v, kc, vc, bias)


def _merge_kernel(a_ref, b_ref, n_ref, hd_ref, wpa_ref, wpb_ref, wpc_ref, wga_ref, wgb_ref, wgc_ref,
                  bga_ref, bgb_ref, bgc_ref, o_ref):
    hd = hd_ref[...]

    def gated(o_ref_, wp_ref, wg_ref, bg_ref):
        z = _dot(hd, wg_ref[...]) + bg_ref[...]
        return (1.0 / (1.0 + jnp.exp(-z))) * _dot(o_ref_[...], wp_ref[...])

    out = gated(a_ref, wpa_ref, wga_ref, bga_ref) + gated(b_ref, wpb_ref, wgb_ref, bgb_ref)
    out = out + gated(n_ref, wpc_ref, wgc_ref, bgc_ref)
    o_ref[...] = out.astype(o_ref.dtype)


def _merge(a, b, n, hd, hd_col, wpa, wpb, wpc, wgu, bg, tm=1024, tn=512):
    m = a.shape[0]
    d = wpa.shape[1]
    rank = wgu.shape[0]
    tm, tn = min(tm, m), min(tn, d)
    assert m % tm == 0 and d % tn == 0 and hd_col % rank == 0
    nj = d // tn
    hb = hd_col // rank

    def act(w):
        return pl.BlockSpec((tm, w), lambda i, j: (i, 0))

    def wt(rows, off):
        return pl.BlockSpec((rows, tn), lambda i, j: (0, off * nj + j))

    return pl.pallas_call(
        _merge_kernel,
        grid=(m // tm, nj),
        in_specs=[act(a.shape[1]), act(b.shape[1]), act(n.shape[1]),
                  pl.BlockSpec((tm, rank), lambda i, j: (i, hb)),
                  wt(wpa.shape[0], 0), wt(wpb.shape[0], 0), wt(wpc.shape[0], 0),
                  wt(rank, 0), wt(rank, 1), wt(rank, 2), wt(1, 0), wt(1, 1), wt(1, 2)],
        out_specs=pl.BlockSpec((tm, tn), lambda i, j: (i, j)),
        out_shape=jax.ShapeDtypeStruct((m, d), BF16),
        compiler_params=_params("arbitrary", "arbitrary"),
        name="branch_merge",
    )(a, b, n, hd, wpa, wpb, wpc, wgu, wgu, wgu, bg, bg, bg)


def _rope_pattern(n_tokens):
    t = np.arange(n_tokens)
    row = jnp.asarray(t // GRID_W, F32)
    col = jnp.asarray(t % GRID_W, F32)
    axis_dim = MLA_ROPE // 2
    inv = ROPE_THETA ** (-jnp.arange(0, axis_dim, 2, dtype=F32) / axis_dim)
    ang_r, ang_c = row[:, None] * inv, col[:, None] * inv
    ang = jnp.concatenate([ang_r, ang_r, ang_c, ang_c], axis=1)
    return jnp.cos(ang), jnp.sin(ang)


def _rotate_half_matrix(width, lanes):
    p = np.zeros((width, width), np.float32)
    q = MLA_ROPE // 4
    for i in lanes:
        if (i % (2 * q)) < q:
            p[i + q, i] = -1.0
        else:
            p[i - q, i] = 1.0
    return p


def _group_matrix(width, group):
    idx = np.arange(width) // group
    return (idx[:, None] == idx[None, :]).astype(np.float32)


def _na_bias_table(rpb):
    case = np.arange(NA_KH)[:, None, None, None]
    qc = np.arange(GRID_W)[None, :, None, None]
    j = np.arange(NA_KH)[None, None, :, None]
    kcol = np.arange(GRID_W)[None, None, None, :]
    start = np.clip(qc - NA_KW // 2, 0, GRID_W - NA_KW)
    valid = np.broadcast_to((kcol >= start) & (kcol < start + NA_KW), (NA_KH, GRID_W, NA_KH, GRID_W))
    dr = np.broadcast_to(case + j, valid.shape)
    dc = np.broadcast_to(np.clip(kcol - qc + NA_KW - 1, 0, 2 * NA_KW - 2), valid.shape)
    bias = rpb.astype(F32)[:, dr, dc] * LOG2E
    bias = jnp.where(valid[None], bias, MASKED)
    return bias.reshape(rpb.shape[0], NA_KH, GRID_W, NA_KH * GRID_W)


def kernel(x, c, ctx, c_ctx, w_ada_down, w_ada_up, b_ada, norm_attn_g, norm_mlp_g, w_in, diff_lambda, diff_q_norm, diff_k_norm, diff_subln, mla_q_a_norm, mla_kv_a_norm, w_mla_uq, w_mla_ukv, mla_q_norm, mla_k_norm, na_q_norm, na_k_norm, na_rpb, w_proj_a, w_proj_b, w_proj_c, w_gate_down, w_gate_up, b_gate, w_out, w_mlp_up, w_mlp_down):
    batch, seq, d = x.shape
    assert batch == 1
    depth = w_in.shape[0]
    hd_n = w_proj_a.shape[1] // DIFF_VD
    hm_n = w_proj_b.shape[1] // MLA_V
    hn_n = w_proj_c.shape[1] // NA_DH
    q_rank = mla_q_a_norm.shape[1]
    kv_rank = mla_kv_a_norm.shape[1]
    g_rank = w_gate_down.shape[2]
    wd, wm, wn = hd_n * DIFF_VD, hm_n * MLA_QK_PAD, hn_n * NA_DH
    kv_in = kv_rank + 2 * MLA_ROPE

    col_dq, col_dk, col_dv, col_cq = 0, wd, 2 * wd, 3 * wd
    col_nq = col_cq + q_rank
    col_nk, col_nv = col_nq + wn, col_nq + 2 * wn
    col_gd, col_kv = 0, g_rank + LANES

    cos64, sin64 = _rope_pattern(seq)
    ones64 = jnp.ones((seq, MLA_ROPE), F32)
    zeros64 = jnp.zeros((seq, MLA_ROPE), F32)
    cos_d, sin_d = jnp.tile(cos64, (1, 2)), jnp.tile(sin64, (1, 2))
    cos_m = jnp.concatenate([ones64, ones64, cos64, ones64], axis=1)
    sin_m = jnp.concatenate([zeros64, zeros64, sin64, zeros64], axis=1)
    pm_d = jnp.asarray(_rotate_half_matrix(LANES, range(LANES)), BF16)
    pm_m = jnp.asarray(_rotate_half_matrix(MLA_QK_PAD, range(MLA_NOPE, MLA_QK)), BF16)
    gm_d = jnp.asarray(_group_matrix(LANES, DIFF_DH), BF16)
    gm_m = jnp.asarray(_group_matrix(MLA_QK_PAD, MLA_QK_PAD), BF16)
    gm_n = jnp.asarray(_group_matrix(LANES, NA_DH), BF16)
    rot_d = np.abs(_rotate_half_matrix(LANES, range(LANES))).argmax(axis=0)

    cond = jnp.zeros((16, d), F32).at[0].set(c[0]).at[1].set(c_ctx)
    mod = _ada(cond, w_ada_down.astype(BF16), w_ada_up.astype(BF16), b_ada)

    def gain_row(g, reps, scale=1.0):
        return jnp.tile(g.astype(F32) * scale, reps)[None, :]

    def pad_heads(g):
        return jnp.concatenate([g.astype(F32), jnp.zeros((MLA_QK_PAD - MLA_QK,), F32)])

    xl, xc = x[0], ctx[0]
    for l in range(depth):
        need_ctx = l < depth - 1
        lam_init = 0.8 - 0.6 * math.exp(-0.3 * l)

        wi = w_in[l]
        sp = np.cumsum([0, wd, wd, wd, q_rank, kv_rank, MLA_ROPE, wn, wn, wn])
        part = [wi[:, sp[i]:sp[i + 1]] for i in range(9)]
        head_major = lambda w: w.reshape(d, 2, hd_n, DIFF_DH).transpose(0, 2, 1, 3).reshape(d, wd)
        w1a = jnp.concatenate([head_major(part[0]), head_major(part[1]), part[2], part[3],
                               part[6], part[7], part[8]], axis=1).astype(BF16)
        w1b = jnp.concatenate([w_gate_down[l], jnp.zeros((d, LANES), F32), part[4], part[5],
                               jnp.zeros((d, MLA_ROPE), F32)], axis=1).astype(BF16)
        uq = w_mla_uq[l].reshape(q_rank, hm_n, MLA_QK)
        w_uq = jnp.concatenate([uq, jnp.zeros((q_rank, hm_n, MLA_QK_PAD - MLA_QK), F32)],
                               axis=2).reshape(q_rank, wm).astype(BF16)
        ukv = w_mla_ukv[l].reshape(kv_rank, hm_n, MLA_NOPE + MLA_V)
        k_rows = jnp.concatenate([ukv[:, :, :MLA_NOPE], jnp.zeros((kv_rank, hm_n, MLA_QK_PAD - MLA_NOPE), F32)], axis=2)
        pe_rows = jnp.concatenate([jnp.zeros((MLA_ROPE, hm_n, MLA_NOPE), F32),
                                   jnp.broadcast_to(jnp.eye(MLA_ROPE, dtype=F32)[:, None, :], (MLA_ROPE, hm_n, MLA_ROPE)),
                                   jnp.zeros((MLA_ROPE, hm_n, MLA_QK_PAD - MLA_QK), F32)], axis=2)
        w_kv = jnp.concatenate([
            jnp.concatenate([k_rows.reshape(kv_rank, wm), ukv[:, :, MLA_NOPE:].reshape(kv_rank, hm_n * MLA_V)], axis=1),
            jnp.concatenate([pe_rows.reshape(MLA_ROPE, wm), jnp.zeros((MLA_ROPE, hm_n * MLA_V), F32)], axis=1),
            jnp.zeros((MLA_ROPE, wm + hm_n * MLA_V), F32)], axis=0).astype(BF16)
        kv_gain = jnp.concatenate([mla_kv_a_norm[l].astype(F32), jnp.ones((2 * MLA_ROPE,), F32)])[None, :]
        wpa, wpb, wpc = w_proj_a[l].astype(BF16), w_proj_b[l].astype(BF16), w_proj_c[l].astype(BF16)
        wgu, bg = w_gate_up[l].astype(BF16), b_gate[l][None, :]
        wo, wup, wdn = w_out[l].astype(BF16), w_mlp_up[l].astype(BF16), w_mlp_down[l].astype(BF16)

        dq_gain = gain_row(diff_q_norm[l], 2, DIFF_DH ** -0.5 * LOG2E)
        dk_gain = gain_row(diff_k_norm[l], 2)
        mq_gain = pad_heads(mla_q_norm[l])[None, :] * (MLA_QK ** -0.5 * LOG2E)
        mk_gain = pad_heads(mla_k_norm[l])[None, :]
        nq_gain = gain_row(na_q_norm[l], 1, NA_DH ** -0.5 * LOG2E)
        nk_gain = gain_row(na_k_norm[l], 1)
        rot_m = np.arange(MLA_QK_PAD)
        rot_m[MLA_NOPE:MLA_QK] = MLA_NOPE + rot_d[:MLA_ROPE]
        na_bias = _na_bias_table(na_rpb[l])
        lam_p = diff_lambda[l].astype(F32)
        sub_g = diff_subln[l].astype(F32)[None, :]

        def front(xs, mrow, rotate):
            m6 = [mod[l, mrow, i * d:(i + 1) * d][None, :] for i in range(N_MOD)]
            h = _modulate(xs, norm_attn_g[l][None, :], m6[0], m6[1])
            pa = _matmul(h, w1a, name="in_proj_a")
            pb = _matmul(h, w1b, tn=w1b.shape[1], name="in_proj_b")

            def rope_args(gain, rot_idx, pmat, cos, sin):
                return (gain[:, rot_idx], pmat, cos, sin) if rotate else None

            tile_d = lambda g: jnp.tile(g, (1, hd_n))
            dq = _headnorm(pa, col_dq, wd, wb=LANES, gain=tile_d(dq_gain), gmat=gm_d, inv_cnt=1.0 / DIFF_DH,
                           rope=rope_args(tile_d(dq_gain), np.tile(rot_d, hd_n) + np.repeat(np.arange(hd_n) * LANES, LANES), pm_d, cos_d, sin_d),
                           name="diff_q_prep")
            dk = _headnorm(pa, col_dk, wd, wb=LANES, gain=tile_d(dk_gain), gmat=gm_d, inv_cnt=1.0 / DIFF_DH,
                           rope=rope_args(tile_d(dk_gain), np.tile(rot_d, hd_n) + np.repeat(np.arange(hd_n) * LANES, LANES), pm_d, cos_d, sin_d),
                           name="diff_k_prep")
            q_raw = _matmul(pa, w_uq, a_col=col_cq, norm_gain=mla_q_a_norm[l].astype(F32)[None, :], n_norm=q_rank,
                            name="mla_q_up")
            kv_raw = _matmul(pb, w_kv, a_col=col_kv, norm_gain=kv_gain, n_norm=kv_rank, name="mla_kv_up")
            tile_m = lambda g: jnp.tile(g, (1, hm_n))
            rot_m_all = np.tile(rot_m, hm_n) + np.repeat(np.arange(hm_n) * MLA_QK_PAD, MLA_QK_PAD)
            mq = _headnorm(q_raw, 0, wm, wb=MLA_QK_PAD, gain=tile_m(mq_gain), gmat=gm_m, inv_cnt=1.0 / MLA_QK,
                           rope=rope_args(tile_m(mq_gain), rot_m_all, pm_m, cos_m, sin_m), tm=256, name="mla_q_prep")
            mk = _headnorm(kv_raw, 0, wm, wb=MLA_QK_PAD, gain=tile_m(mk_gain), gmat=gm_m, inv_cnt=1.0 / MLA_QK,
                           rope=rope_args(tile_m(mk_gain), rot_m_all, pm_m, cos_m, sin_m), tm=256, name="mla_k_prep")
            tile_n = lambda g: jnp.tile(g, (1, hn_n))
            nq = _headnorm(pa, col_nq, wn, wb=LANES, gain=tile_n(nq_gain), gmat=gm_n, inv_cnt=1.0 / NA_DH, name="na_q_prep")
            nk = _headnorm(pa, col_nk, wn, wb=LANES, gain=tile_n(nk_gain), gmat=gm_n, inv_cnt=1.0 / NA_DH, name="na_k_prep")
            return dict(m6=m6, h=h, pa=pa, pb=pb, dq=dq, dk=dk, mq=mq, mk=mk, kv_raw=kv_raw, nq=nq, nk=nk)

        def back(xs, f, o_a, o_b, o_c):
            m6 = f["m6"]
            merged = _merge(o_a, o_b, o_c, f["pb"], col_gd, wpa, wpb, wpc, wgu, bg)
            xs = _matmul(merged, wo, out_dtype=F32, epilogue="resid", resid=xs, gate=m6[2], tn=512, name="out_proj")
            h2 = _modulate(xs, norm_mlp_g[l][None, :], m6[3], m6[4])
            hid = _matmul(h2, wup, epilogue="relu2", name="mlp_up")
            return _matmul(hid, wdn, out_dtype=F32, epilogue="resid", resid=xs, gate=m6[5], tk=2048, name="mlp_down")

        fc = front(xc, 1, False)
        fl = front(xl, 0, True)
        dv_c, dv_l = fc["pa"][:, col_dv:col_dv + wd], fl["pa"][:, col_dv:col_dv + wd]
        mv_c, mv_l = fc["kv_raw"][:, wm:], fl["kv_raw"][:, wm:]
        dk_all = jnp.concatenate([fc["dk"], fl["dk"]], axis=0)
        dv_all = jnp.concatenate([dv_c, dv_l], axis=0)
        mk_all = jnp.concatenate([fc["mk"], fl["mk"]], axis=0)
        mv_all = jnp.concatenate([mv_c, mv_l], axis=0)
        diff_args = (lam_p, sub_g, lam_init)

        a_l = _flash(fl["dq"], dk_all, dv_all, heads=hd_n, dqk=DIFF_VD, diff=diff_args, tq=512, name="diff_attn")
        b_l = _flash(fl["mq"], mk_all, mv_all, heads=hm_n, dqk=MLA_QK_PAD, tq=1024, name="mla_attn")
        n_l = _na_attention(fl["nq"], fl["nk"], fl["pa"], col_nv, fc["nk"], fc["pa"], col_nv, na_bias, heads=hn_n)
        xl = back(xl, fl, a_l, b_l, n_l)
        if need_ctx:
            a_c = _flash(fc["dq"], fc["dk"], fc["pa"], v_col=col_dv, heads=hd_n, dqk=DIFF_VD, diff=diff_args,
                         name="diff_attn_ctx")
            b_c = _flash(fc["mq"], fc["mk"], fc["kv_raw"], v_col=wm, heads=hm_n, dqk=MLA_QK_PAD, name="mla_attn_ctx")
            n_c = _flash(fc["nq"], fc["nk"], fc["pa"], v_col=col_nv, heads=hn_n, dqk=NA_DH, name="na_attn_ctx")
            xc = back(xc, fc, a_c, b_c, n_c)
    return xl[None]
```

```python
import functools
import math

import numpy as np
import jax
import jax.numpy as jnp
from jax import lax
from jax.experimental import pallas as pl
from jax.experimental.pallas import tpu as pltpu

F32 = jnp.float32
BF16 = jnp.bfloat16

NORM_EPS = 1e-6
ROPE_THETA = 10000.0
GRID_W = 64
DIFF_DH = 64
DIFF_VD = 2 * DIFF_DH
MLA_NOPE = 128
MLA_ROPE = 64
MLA_QK = MLA_NOPE + MLA_ROPE
MLA_QK_PAD = 256
MLA_V = 128
NA_DH = 128
NA_KH = 8
NA_KW = 16
NA_QROWS = 4
NA_WROWS = 12
N_MOD = 6
N_BRANCH = 3
LOG2E = 1.4426950408889634
MASKED = -1e30
LANES = 128
ROW_CHUNK = 16
VMEM_LIMIT_BYTES = 56 * 1024 * 1024


def _params(*semantics):
    return pltpu.CompilerParams(dimension_semantics=semantics, vmem_limit_bytes=VMEM_LIMIT_BYTES)


def _dot(a, b):
    return jnp.dot(a, b, preferred_element_type=F32)


def _dot_nt(a, b):
    return lax.dot_general(a, b, (((1,), (1,)), ((), ())), preferred_element_type=F32)


def _ada_kernel(cond_ref, wd_ref, wu_ref, b_ref, o_ref):
    cnd = cond_ref[...]
    act = cnd / (1.0 + jnp.exp(-cnd))
    low = _dot(act.astype(BF16), wd_ref[...])
    o_ref[0] = _dot(low.astype(BF16), wu_ref[0]) + b_ref[0]


def _ada(cond, w_down, w_up, b):
    rows, d = cond.shape
    depth, rank, n = w_up.shape
    tn = min(n, 4096)
    return pl.pallas_call(
        _ada_kernel,
        grid=(depth, n // tn),
        in_specs=[pl.BlockSpec((rows, d), lambda l, j: (0, 0)),
                  pl.BlockSpec((d, rank), lambda l, j: (0, 0)),
                  pl.BlockSpec((1, rank, tn), lambda l, j: (l, 0, j)),
                  pl.BlockSpec((1, 1, tn), lambda l, j: (l, 0, j))],
        out_specs=pl.BlockSpec((1, rows, tn), lambda l, j: (l, 0, j)),
        out_shape=jax.ShapeDtypeStruct((depth, rows, n), F32),
        compiler_params=_params("arbitrary", "arbitrary"),
        name="ada_mod",
    )(cond, w_down, w_up, b.reshape(depth, 1, n))


def _modulate_kernel(x_ref, g_ref, sh_ref, sc_ref, o_ref):
    x = x_ref[...]
    r = lax.rsqrt(jnp.mean(x * x, axis=-1, keepdims=True) + NORM_EPS)
    y = (x * r) * g_ref[...]
    o_ref[...] = (y * (1.0 + sc_ref[...]) + sh_ref[...]).astype(o_ref.dtype)


def _modulate(x, g, shift, scale, tm=256):
    m, d = x.shape
    tm = min(tm, m)
    vec = pl.BlockSpec((1, d), lambda i: (0, 0))
    return pl.pallas_call(
        _modulate_kernel,
        grid=(m // tm,),
        in_specs=[pl.BlockSpec((tm, d), lambda i: (i, 0)), vec, vec, vec],
        out_specs=pl.BlockSpec((tm, d), lambda i: (i, 0)),
        out_shape=jax.ShapeDtypeStruct((m, d), BF16),
        compiler_params=_params("arbitrary"),
        name="modulate",
    )(x, g, shift, scale)


def _mm_kernel(*refs, nk, n_norm, epilogue):
    it = iter(refs)
    a_ref, w_ref = next(it), next(it)
    g_ref = next(it) if n_norm else None
    x_ref, gate_ref = (next(it), next(it)) if epilogue == "resid" else (None, None)
    o_ref = next(it)
    acc_ref = next(it) if nk > 1 else None

    a = a_ref[...]
    if n_norm:
        af = a.astype(F32)
        if n_norm < af.shape[1]:
            normed = lax.broadcasted_iota(jnp.int32, af.shape, 1) < n_norm
            ssq = jnp.sum(jnp.where(normed, af * af, 0.0), axis=-1, keepdims=True)
            r = lax.rsqrt(ssq * (1.0 / n_norm) + NORM_EPS)
            a = jnp.where(normed, (af * r) * g_ref[...], af).astype(BF16)
        else:
            r = lax.rsqrt(jnp.mean(af * af, axis=-1, keepdims=True) + NORM_EPS)
            a = ((af * r) * g_ref[...]).astype(BF16)
    part = _dot(a, w_ref[...])

    def finish(acc):
        if epilogue == "relu2":
            acc = jnp.maximum(acc, 0.0)
            acc = acc * acc
        elif epilogue == "resid":
            acc = x_ref[...] + gate_ref[...] * acc
        o_ref[...] = acc.astype(o_ref.dtype)

    if nk == 1:
        finish(part)
    else:
        k = pl.program_id(2)

        @pl.when(k == 0)
        def _():
            acc_ref[...] = part

        @pl.when(k > 0)
        def _():
            acc_ref[...] += part

        @pl.when(k == nk - 1)
        def _():
            finish(acc_ref[...])


def _matmul(a, w, *, a_col=0, out_dtype=BF16, norm_gain=None, n_norm=0, epilogue=None,
            resid=None, gate=None, tm=1024, tn=1024, tk=None, name="matmul"):
    m = a.shape[0]
    k_dim, n = w.shape
    tm, tn, tk = min(tm, m), min(tn, n), min(tk or k_dim, k_dim)
    assert m % tm == 0 and n % tn == 0 and k_dim % tk == 0 and a_col % tk == 0, (m, n, k_dim, tm, tn, tk, a_col)
    nk = k_dim // tk
    a_off = a_col // tk
    assert not (n_norm and nk > 1)
    in_specs = [pl.BlockSpec((tm, tk), lambda i, j, k: (i, a_off + k)),
                pl.BlockSpec((tk, tn), lambda i, j, k: (k, j))]
    args = [a, w]
    if n_norm:
        in_specs.append(pl.BlockSpec((1, tk), lambda i, j, k: (0, 0)))
        args.append(norm_gain)
    aliases = {}
    if epilogue == "resid":
        aliases = {len(args): 0}
        in_specs += [pl.BlockSpec((tm, tn), lambda i, j, k: (i, j)),
                     pl.BlockSpec((1, tn), lambda i, j, k: (0, j))]
        args += [resid, gate]
    return pl.pallas_call(
        functools.partial(_mm_kernel, nk=nk, n_norm=n_norm, epilogue=epilogue),
        grid=(m // tm, n // tn, nk),
        in_specs=in_specs,
        out_specs=pl.BlockSpec((tm, tn), lambda i, j, k: (i, j)),
        out_shape=jax.ShapeDtypeStruct((m, n), out_dtype),
        scratch_shapes=[pltpu.VMEM((tm, tn), F32)] if nk > 1 else [],
        input_output_aliases=aliases,
        compiler_params=_params("arbitrary", "arbitrary", "arbitrary"),
        name=name,
    )(*args)


def _headnorm_kernel(*refs, wb, nchunk, inv_cnt, rope):
    if rope:
        x_ref, g_ref, gm_ref, gr_ref, pm_ref, cos_ref, sin_ref, o_ref = refs
    else:
        x_ref, g_ref, gm_ref, o_ref = refs
    for c in range(nchunk):
        cols = slice(c * wb, (c + 1) * wb)
        xb = x_ref[:, cols]
        xf = xb.astype(F32)
        sq = xf * xf
        hi = sq.astype(BF16)
        lo = (sq - hi.astype(F32)).astype(BF16)
        ssq = _dot(hi, gm_ref[...]) + _dot(lo, gm_ref[...])
        r = lax.rsqrt(ssq * inv_cnt + NORM_EPS)
        y = (xf * r) * g_ref[:, cols]
        if rope:
            yr = (_dot(xb, pm_ref[...]) * r) * gr_ref[:, cols]
            y = y * cos_ref[...] + yr * sin_ref[...]
        o_ref[:, cols] = y.astype(o_ref.dtype)


def _headnorm(x, col, width, *, wb, gain, gmat, inv_cnt, rope=None, tm=512, name="headnorm"):
    m = x.shape[0]
    tm = min(tm, m)
    assert m % tm == 0 and col % width == 0 and width % wb == 0
    cb = col // width
    row = pl.BlockSpec((1, width), lambda i: (0, 0))
    mat = pl.BlockSpec((wb, wb), lambda i: (0, 0))
    in_specs = [pl.BlockSpec((tm, width), lambda i: (i, cb)), row, mat]
    args = [x, gain, gmat]
    if rope is not None:
        gain_rot, pmat, cos, sin = rope
        tab = pl.BlockSpec((tm, wb), lambda i: (i, 0))
        in_specs += [row, mat, tab, tab]
        args += [gain_rot, pmat, cos, sin]
    return pl.pallas_call(
        functools.partial(_headnorm_kernel, wb=wb, nchunk=width // wb, inv_cnt=inv_cnt, rope=rope is not None),
        grid=(m // tm,),
        in_specs=in_specs,
        out_specs=pl.BlockSpec((tm, width), lambda i: (i, 0)),
        out_shape=jax.ShapeDtypeStruct((m, width), BF16),
        compiler_params=_params("arbitrary"),
        name=name,
    )(*args)


def _flash_kernel(*refs, diff, tq, tk, n, lam_init):
    if diff:
        q_ref, k_ref, v_ref, lp_ref, sg_ref, o_ref, qs_ref, s0, s1, p0, p1, a0, a1, m_ref, acc_ref = refs
    else:
        q_ref, k_ref, v_ref, o_ref, s0, s1, p0, p1, a0, a1, m_ref, acc_ref = refs
        qs_ref = q_ref
    s_slots, p_slots, a_slots = (s0, s1), (p0, p1), (a0, a1)
    rows = m_ref.shape[0]

    m_ref[...] = jnp.full(m_ref.shape, MASKED, F32)
    acc_ref[...] = jnp.zeros(acc_ref.shape, F32)
    if diff:
        q = q_ref[...].astype(F32)
        first = lax.broadcasted_iota(jnp.int32, q.shape, 1) < DIFF_DH
        qs_ref[0:tq, :] = jnp.where(first, q, 0.0).astype(BF16)
        qs_ref[tq:2 * tq, :] = jnp.where(first, 0.0, q).astype(BF16)

    def key_rows(j):
        off = j * tk
        return pl.ds(off if isinstance(j, int) else pl.multiple_of(off, 2 * LANES), tk)

    def scores(j, slot):
        s_slots[slot][...] = _dot_nt(qs_ref[...], k_ref[key_rows(j), :])

    def softmax(slot):
        s_ref, p_ref, a_ref = s_slots[slot], p_slots[slot], a_slots[slot]
        for r in range(0, rows, ROW_CHUNK):
            rs = slice(r, r + ROW_CHUNK)
            m_prev = m_ref[rs, :]
            m_new = jnp.maximum(m_prev, jnp.max(s_ref[rs, :], axis=1, keepdims=True))
            a_ref[rs, :] = jnp.exp2(m_prev - m_new)
            m_ref[rs, :] = m_new
            p_ref[rs, :] = jnp.exp2(s_ref[rs, :] - m_new[:, :1]).astype(BF16)

    def values(j, slot):
        alpha = a_slots[slot][...]
        acc_ref[...] = (jnp.concatenate([alpha, alpha], axis=1) * acc_ref[...]
                        + _dot(p_slots[slot][...], v_ref[key_rows(j), :]))

    def step(j, parity, do_scores, do_softmax, do_values):
        if do_scores:
            scores(j, parity)
        if do_softmax:
            softmax(1 - parity)
        if do_values:
            values(j - 2, parity)

    pairs = max(0, (n - 2) // 2)
    for j in range(min(2, n + 2)):
        step(j, j % 2, j < n, 1 <= j <= n, False)

    def pair(i, carry):
        j = 2 + 2 * i
        step(j, 0, True, True, True)
        step(j + 1, 1, True, True, True)
        return carry

    if pairs:
        lax.fori_loop(0, pairs, pair, 0)
    for j in range(2 + 2 * pairs, n + 2):
        step(j, j % 2, j < n, j <= n, True)

    acc = acc_ref[...]
    o = acc[:, :LANES] / acc[:, LANES:]
    if diff:
        lp = lp_ref[...]
        lam = (jnp.exp(jnp.sum(lp[0:1] * lp[1:2], axis=1, keepdims=True))
               - jnp.exp(jnp.sum(lp[2:3] * lp[3:4], axis=1, keepdims=True)) + lam_init)
        d = o[0:tq] - lam * o[tq:2 * tq]
        r = lax.rsqrt(jnp.mean(d * d, axis=-1, keepdims=True) + NORM_EPS)
        o = ((d * r) * sg_ref[...]) * (1.0 - lam_init)
    o_ref[...] = o.astype(o_ref.dtype)


def _with_ones(v, heads):
    v3 = v.reshape(v.shape[0], heads, LANES)
    return jnp.concatenate([v3, jnp.ones_like(v3)], axis=2).reshape(v.shape[0], 2 * heads * LANES)


def _flash(q, k, v_ones, *, heads, dqk, k_col=0, diff=None, tq=512, tk=1280, name="flash"):
    m = q.shape[0]
    skv = k.shape[0]
    tq, tk = min(tq, m), min(tk, skv)
    assert m % tq == 0 and skv % tk == 0 and k_col % dqk == 0 and tk % (2 * LANES) == 0
    kb = k_col // dqk
    rows = 2 * tq if diff else tq
    assert rows % ROW_CHUNK == 0
    in_specs = [pl.BlockSpec((tq, dqk), lambda h, i: (i, h)),
                pl.BlockSpec((skv, dqk), lambda h, i: (0, kb + h)),
                pl.BlockSpec((skv, 2 * LANES), lambda h, i: (0, h))]
    args = [q, k, v_ones]
    scratch = []
    lam_init = 0.0
    if diff:
        lam_p, sub_g, lam_init = diff
        in_specs += [pl.BlockSpec(lam_p.shape, lambda h, i: (0, 0)),
                     pl.BlockSpec((1, LANES), lambda h, i: (0, 0))]
        args += [lam_p, sub_g]
        scratch.append(pltpu.VMEM((rows, dqk), BF16))
    scratch += [pltpu.VMEM((rows, tk), F32), pltpu.VMEM((rows, tk), F32),
                pltpu.VMEM((rows, tk), BF16), pltpu.VMEM((rows, tk), BF16),
                pltpu.VMEM((rows, LANES), F32), pltpu.VMEM((rows, LANES), F32),
                pltpu.VMEM((rows, LANES), F32), pltpu.VMEM((rows, 2 * LANES), F32)]
    return pl.pallas_call(
        functools.partial(_flash_kernel, diff=bool(diff), tq=tq, tk=tk, n=skv // tk, lam_init=lam_init),
        grid=(heads, m // tq),
        in_specs=in_specs,
        out_specs=pl.BlockSpec((tq, LANES), lambda h, i: (i, h)),
        out_shape=jax.ShapeDtypeStruct((m, heads * LANES), BF16),
        scratch_shapes=scratch,
        compiler_params=_params("arbitrary", "arbitrary"),
        name=name,
    )(*args)


def _na_kernel(q_ref, k_ref, v_ref, kc_ref, vc_ref, b_ref, o_ref, *, nb, rows):
    step = pl.program_id(1)
    kc = kc_ref[...]
    vc = vc_ref[...]
    tq, win = NA_QROWS * GRID_W, NA_WROWS * GRID_W
    last = rows // NA_QROWS - 1

    def one_block(bb, carry):
        b = step * nb + bb
        first_row = jnp.clip(b * NA_QROWS - NA_KH // 2, 0, rows - NA_WROWS)
        case = jnp.where(b == 0, 0, jnp.where(b == last, 2, 1))
        q_rows = pl.ds(pl.multiple_of(bb * tq, tq), tq)
        k_rows = pl.ds(pl.multiple_of(first_row * GRID_W, tq), win)
        q = q_ref[q_rows, :]
        s_win = _dot_nt(q, k_ref[k_rows, :]) + b_ref[0, case]
        s_ctx = _dot_nt(q, kc)
        m = jnp.maximum(jnp.max(s_win, axis=1, keepdims=True), jnp.max(s_ctx, axis=1, keepdims=True))
        p_win = jnp.exp2(s_win - m)
        p_ctx = jnp.exp2(s_ctx - m)
        l = jnp.sum(p_win, axis=1, keepdims=True) + jnp.sum(p_ctx, axis=1, keepdims=True)
        o = _dot(p_win.astype(BF16), v_ref[k_rows, :]) + _dot(p_ctx.astype(BF16), vc)
        o_ref[q_rows, :] = (o / l).astype(o_ref.dtype)
        return carry

    lax.fori_loop(0, nb, one_block, 0)


def _na_attention(q, k, v, v_col, kc, vc, vc_col, bias, *, heads, nb=4):
    s = q.shape[0]
    rows = s // GRID_W
    nb = min(nb, rows // NA_QROWS)
    assert s % GRID_W == 0 and rows >= NA_WROWS and rows % (NA_QROWS * nb) == 0
    nctx = kc.shape[0]
    vb, vcb = v_col // NA_DH, vc_col // NA_DH
    tq = nb * NA_QROWS * GRID_W
    return pl.pallas_call(
        functools.partial(_na_kernel, nb=nb, rows=rows),
        grid=(heads, s // tq),
        in_specs=[pl.BlockSpec((tq, NA_DH), lambda h, i: (i, h)),
                  pl.BlockSpec((s, NA_DH), lambda h, i: (0, h)),
                  pl.BlockSpec((s, NA_DH), lambda h, i: (0, vb + h)),
                  pl.BlockSpec((nctx, NA_DH), lambda h, i: (0, h)),
                  pl.BlockSpec((nctx, NA_DH), lambda h, i: (0, vcb + h)),
                  pl.BlockSpec((1,) + bias.shape[1:], lambda h, i: (h, 0, 0, 0))],
        out_specs=pl.BlockSpec((tq, NA_DH), lambda h, i: (i, h)),
        out_shape=jax.ShapeDtypeStruct((s, heads * NA_DH), BF16),
        compiler_params=_params("arbitrary", "arbitrary"),
        name="na_attention",
    )(q, k, v, kc, vc, bias)


def _merge_kernel(a_ref, b_ref, n_ref, hd_ref, wpa_ref, wpb_ref, wpc_ref, wga_ref, wgb_ref, wgc_ref,
                  bga_ref, bgb_ref, bgc_ref, o_ref):
    hd = hd_ref[...]

    def gated(o_ref_, wp_ref, wg_ref, bg_ref):
        z = _dot(hd, wg_ref[...]) + bg_ref[...]
        return (1.0 / (1.0 + jnp.exp(-z))) * _dot(o_ref_[...], wp_ref[...])

    out = gated(a_ref, wpa_ref, wga_ref, bga_ref) + gated(b_ref, wpb_ref, wgb_ref, bgb_ref)
    out = out + gated(n_ref, wpc_ref, wgc_ref, bgc_ref)
    o_ref[...] = out.astype(o_ref.dtype)


def _merge(a, b, n, hd, hd_col, wpa, wpb, wpc, wgu, bg, tm=1024, tn=512):
    m = a.shape[0]
    d = wpa.shape[1]
    rank = wgu.shape[0]
    tm, tn = min(tm, m), min(tn, d)
    assert m % tm == 0 and d % tn == 0 and hd_col % rank == 0
    nj = d // tn
    hb = hd_col // rank

    def act(w):
        return pl.BlockSpec((tm, w), lambda i, j: (i, 0))

    def wt(rows, off):
        return pl.BlockSpec((rows, tn), lambda i, j: (0, off * nj + j))

    return pl.pallas_call(
        _merge_kernel,
        grid=(m // tm, nj),
        in_specs=[act(a.shape[1]), act(b.shape[1]), act(n.shape[1]),
                  pl.BlockSpec((tm, rank), lambda i, j: (i, hb)),
                  wt(wpa.shape[0], 0), wt(wpb.shape[0], 0), wt(wpc.shape[0], 0),
                  wt(rank, 0), wt(rank, 1), wt(rank, 2), wt(1, 0), wt(1, 1), wt(1, 2)],
        out_specs=pl.BlockSpec((tm, tn), lambda i, j: (i, j)),
        out_shape=jax.ShapeDtypeStruct((m, d), BF16),
        compiler_params=_params("arbitrary", "arbitrary"),
        name="branch_merge",
    )(a, b, n, hd, wpa, wpb, wpc, wgu, wgu, wgu, bg, bg, bg)


def _rope_pattern(n_tokens):
    t = np.arange(n_tokens)
    row = jnp.asarray(t // GRID_W, F32)
    col = jnp.asarray(t % GRID_W, F32)
    axis_dim = MLA_ROPE // 2
    inv = ROPE_THETA ** (-jnp.arange(0, axis_dim, 2, dtype=F32) / axis_dim)
    ang_r, ang_c = row[:, None] * inv, col[:, None] * inv
    ang = jnp.concatenate([ang_r, ang_r, ang_c, ang_c], axis=1)
    return jnp.cos(ang), jnp.sin(ang)


def _rotate_half_matrix(width, lanes):
    p = np.zeros((width, width), np.float32)
    q = MLA_ROPE // 4
    for i in lanes:
        if (i % (2 * q)) < q:
            p[i + q, i] = -1.0
        else:
            p[i - q, i] = 1.0
    return p


def _group_matrix(width, group):
    idx = np.arange(width) // group
    return (idx[:, None] == idx[None, :]).astype(np.float32)


def _na_bias_table(rpb):
    heads = rpb.shape[0]
    edge = GRID_W - NA_KW
    rp = jnp.pad(rpb.astype(F32), ((0, 0), (NA_WROWS - 1, NA_WROWS - 1), (edge, edge)))
    by_col = jnp.stack([rp[:, :, GRID_W - 1 - q:2 * GRID_W - 1 - q] for q in range(GRID_W)], axis=2)
    qc = np.arange(GRID_W)[:, None]
    kcol = np.arange(GRID_W)[None, :]
    start = np.clip(qc - NA_KW // 2, 0, edge)
    valid_col = (kcol >= start) & (kcol < start + NA_KW)
    jr = np.arange(NA_WROWS)
    half = NA_KH // 2
    cases = [[(i, 0) for i in range(NA_QROWS)],
             [(half + i, i) for i in range(NA_QROWS)],
             [(NA_KH + i, NA_WROWS - NA_KH) for i in range(NA_QROWS)]]
    tables = []
    for case in cases:
        per_row = []
        for dq, dr0 in case:
            lo = NA_WROWS - 1 + NA_KH - 1 - dq
            valid = ((jr >= dr0) & (jr < dr0 + NA_KH))[:, None, None] & valid_col[None]
            tile = jnp.where(valid[None], by_col[:, lo:lo + NA_WROWS] * LOG2E, MASKED)
            per_row.append(tile.transpose(0, 2, 1, 3).reshape(heads, GRID_W, NA_WROWS * GRID_W))
        tables.append(jnp.concatenate(per_row, axis=1))
    return jnp.stack(tables, axis=1)


def kernel(x, c, ctx, c_ctx, w_ada_down, w_ada_up, b_ada, norm_attn_g, norm_mlp_g, w_in, diff_lambda, diff_q_norm, diff_k_norm, diff_subln, mla_q_a_norm, mla_kv_a_norm, w_mla_uq, w_mla_ukv, mla_q_norm, mla_k_norm, na_q_norm, na_k_norm, na_rpb, w_proj_a, w_proj_b, w_proj_c, w_gate_down, w_gate_up, b_gate, w_out, w_mlp_up, w_mlp_down):
    batch, seq, d = x.shape
    assert batch == 1
    depth = w_in.shape[0]
    hd_n = w_proj_a.shape[1] // DIFF_VD
    hm_n = w_proj_b.shape[1] // MLA_V
    hn_n = w_proj_c.shape[1] // NA_DH
    q_rank = mla_q_a_norm.shape[1]
    kv_rank = mla_kv_a_norm.shape[1]
    g_rank = w_gate_down.shape[2]
    wd, wm, wn = hd_n * DIFF_VD, hm_n * MLA_QK_PAD, hn_n * NA_DH
    kv_in = kv_rank + 2 * MLA_ROPE

    col_dq, col_dk, col_dv, col_cq = 0, wd, 2 * wd, 3 * wd
    col_nq = col_cq + q_rank
    col_nk, col_nv = col_nq + wn, col_nq + 2 * wn
    col_gd, col_kv = 0, g_rank + LANES

    cos64, sin64 = _rope_pattern(seq)
    ones64 = jnp.ones((seq, MLA_ROPE), F32)
    zeros64 = jnp.zeros((seq, MLA_ROPE), F32)
    cos_d, sin_d = jnp.tile(cos64, (1, 2)), jnp.tile(sin64, (1, 2))
    cos_m = jnp.concatenate([ones64, ones64, cos64, ones64], axis=1)
    sin_m = jnp.concatenate([zeros64, zeros64, sin64, zeros64], axis=1)
    pm_d = jnp.asarray(_rotate_half_matrix(LANES, range(LANES)), BF16)
    pm_m = jnp.asarray(_rotate_half_matrix(MLA_QK_PAD, range(MLA_NOPE, MLA_QK)), BF16)
    gm_d = jnp.asarray(_group_matrix(LANES, DIFF_DH), BF16)
    gm_m = jnp.asarray(_group_matrix(MLA_QK_PAD, MLA_QK_PAD), BF16)
    gm_n = jnp.asarray(_group_matrix(LANES, NA_DH), BF16)
    rot_d = np.abs(_rotate_half_matrix(LANES, range(LANES))).argmax(axis=0)

    cond = jnp.zeros((16, d), F32).at[0].set(c[0]).at[1].set(c_ctx)
    mod = _ada(cond, w_ada_down.astype(BF16), w_ada_up.astype(BF16), b_ada)

    def gain_row(g, reps, scale=1.0):
        return jnp.tile(g.astype(F32) * scale, reps)[None, :]

    def pad_heads(g):
        return jnp.concatenate([g.astype(F32), jnp.zeros((MLA_QK_PAD - MLA_QK,), F32)])

    xl, xc = x[0], ctx[0]
    for l in range(depth):
        need_ctx = l < depth - 1
        lam_init = 0.8 - 0.6 * math.exp(-0.3 * l)

        wi = w_in[l]
        sp = np.cumsum([0, wd, wd, wd, q_rank, kv_rank, MLA_ROPE, wn, wn, wn])
        part = [wi[:, sp[i]:sp[i + 1]] for i in range(9)]
        head_major = lambda w: w.reshape(d, 2, hd_n, DIFF_DH).transpose(0, 2, 1, 3).reshape(d, wd)
        w1a = jnp.concatenate([head_major(part[0]), head_major(part[1]), part[2], part[3],
                               part[6], part[7], part[8]], axis=1).astype(BF16)
        w1b = jnp.concatenate([w_gate_down[l], jnp.zeros((d, LANES), F32), part[4], part[5],
                               jnp.zeros((d, MLA_ROPE), F32)], axis=1).astype(BF16)
        uq = w_mla_uq[l].reshape(q_rank, hm_n, MLA_QK)
        w_uq = jnp.concatenate([uq, jnp.zeros((q_rank, hm_n, MLA_QK_PAD - MLA_QK), F32)],
                               axis=2).reshape(q_rank, wm).astype(BF16)
        ukv = w_mla_ukv[l].reshape(kv_rank, hm_n, MLA_NOPE + MLA_V)
        k_rows = jnp.concatenate([ukv[:, :, :MLA_NOPE], jnp.zeros((kv_rank, hm_n, MLA_QK_PAD - MLA_NOPE), F32)], axis=2)
        pe_rows = jnp.concatenate([jnp.zeros((MLA_ROPE, hm_n, MLA_NOPE), F32),
                                   jnp.broadcast_to(jnp.eye(MLA_ROPE, dtype=F32)[:, None, :], (MLA_ROPE, hm_n, MLA_ROPE)),
                                   jnp.zeros((MLA_ROPE, hm_n, MLA_QK_PAD - MLA_QK), F32)], axis=2)
        w_kv = jnp.concatenate([
            jnp.concatenate([k_rows.reshape(kv_rank, wm), ukv[:, :, MLA_NOPE:].reshape(kv_rank, hm_n * MLA_V)], axis=1),
            jnp.concatenate([pe_rows.reshape(MLA_ROPE, wm), jnp.zeros((MLA_ROPE, hm_n * MLA_V), F32)], axis=1),
            jnp.zeros((MLA_ROPE, wm + hm_n * MLA_V), F32)], axis=0).astype(BF16)
        kv_gain = jnp.concatenate([mla_kv_a_norm[l].astype(F32), jnp.ones((2 * MLA_ROPE,), F32)])[None, :]
        wpa, wpb, wpc = w_proj_a[l].astype(BF16), w_proj_b[l].astype(BF16), w_proj_c[l].astype(BF16)
        wgu, bg = w_gate_up[l].astype(BF16), b_gate[l][None, :]
        wo, wup, wdn = w_out[l].astype(BF16), w_mlp_up[l].astype(BF16), w_mlp_down[l].astype(BF16)

        dq_gain = gain_row(diff_q_norm[l], 2, DIFF_DH ** -0.5 * LOG2E)
        dk_gain = gain_row(diff_k_norm[l], 2)
        mq_gain = pad_heads(mla_q_norm[l])[None, :] * (MLA_QK ** -0.5 * LOG2E)
        mk_gain = pad_heads(mla_k_norm[l])[None, :]
        nq_gain = gain_row(na_q_norm[l], 1, NA_DH ** -0.5 * LOG2E)
        nk_gain = gain_row(na_k_norm[l], 1)
        rot_m = np.arange(MLA_QK_PAD)
        rot_m[MLA_NOPE:MLA_QK] = MLA_NOPE + rot_d[:MLA_ROPE]
        na_bias = _na_bias_table(na_rpb[l])
        lam_p = diff_lambda[l].astype(F32)
        sub_g = diff_subln[l].astype(F32)[None, :]

        def front(xs, mrow, rotate):
            m6 = [mod[l, mrow, i * d:(i + 1) * d][None, :] for i in range(N_MOD)]
            h = _modulate(xs, norm_attn_g[l][None, :], m6[0], m6[1])
            pa = _matmul(h, w1a, name="in_proj_a")
            pb = _matmul(h, w1b, tn=w1b.shape[1], name="in_proj_b")

            def rope_args(gain, rot_idx, pmat, cos, sin):
                return (gain[:, rot_idx], pmat, cos, sin) if rotate else None

            tile_d = lambda g: jnp.tile(g, (1, hd_n))
            dq = _headnorm(pa, col_dq, wd, wb=LANES, gain=tile_d(dq_gain), gmat=gm_d, inv_cnt=1.0 / DIFF_DH,
                           rope=rope_args(tile_d(dq_gain), np.tile(rot_d, hd_n) + np.repeat(np.arange(hd_n) * LANES, LANES), pm_d, cos_d, sin_d),
                           name="diff_q_prep")
            dk = _headnorm(pa, col_dk, wd, wb=LANES, gain=tile_d(dk_gain), gmat=gm_d, inv_cnt=1.0 / DIFF_DH,
                           rope=rope_args(tile_d(dk_gain), np.tile(rot_d, hd_n) + np.repeat(np.arange(hd_n) * LANES, LANES), pm_d, cos_d, sin_d),
                           name="diff_k_prep")
            q_raw = _matmul(pa, w_uq, a_col=col_cq, norm_gain=mla_q_a_norm[l].astype(F32)[None, :], n_norm=q_rank,
                            name="mla_q_up")
            kv_raw = _matmul(pb, w_kv, a_col=col_kv, norm_gain=kv_gain, n_norm=kv_rank, name="mla_kv_up")
            tile_m = lambda g: jnp.tile(g, (1, hm_n))
            rot_m_all = np.tile(rot_m, hm_n) + np.repeat(np.arange(hm_n) * MLA_QK_PAD, MLA_QK_PAD)
            mq = _headnorm(q_raw, 0, wm, wb=MLA_QK_PAD, gain=tile_m(mq_gain), gmat=gm_m, inv_cnt=1.0 / MLA_QK,
                           rope=rope_args(tile_m(mq_gain), rot_m_all, pm_m, cos_m, sin_m), tm=256, name="mla_q_prep")
            mk = _headnorm(kv_raw, 0, wm, wb=MLA_QK_PAD, gain=tile_m(mk_gain), gmat=gm_m, inv_cnt=1.0 / MLA_QK,
                           rope=rope_args(tile_m(mk_gain), rot_m_all, pm_m, cos_m, sin_m), tm=256, name="mla_k_prep")
            tile_n = lambda g: jnp.tile(g, (1, hn_n))
            nq = _headnorm(pa, col_nq, wn, wb=LANES, gain=tile_n(nq_gain), gmat=gm_n, inv_cnt=1.0 / NA_DH, name="na_q_prep")
            nk = _headnorm(pa, col_nk, wn, wb=LANES, gain=tile_n(nk_gain), gmat=gm_n, inv_cnt=1.0 / NA_DH, name="na_k_prep")
            return dict(m6=m6, h=h, pa=pa, pb=pb, dq=dq, dk=dk, mq=mq, mk=mk, kv_raw=kv_raw, nq=nq, nk=nk)

        def back(xs, f, o_a, o_b, o_c):
            m6 = f["m6"]
            merged = _merge(o_a, o_b, o_c, f["pb"], col_gd, wpa, wpb, wpc, wgu, bg)
            xs = _matmul(merged, wo, out_dtype=F32, epilogue="resid", resid=xs, gate=m6[2], tn=512, name="out_proj")
            h2 = _modulate(xs, norm_mlp_g[l][None, :], m6[3], m6[4])
            hid = _matmul(h2, wup, epilogue="relu2", name="mlp_up")
            return _matmul(hid, wdn, out_dtype=F32, epilogue="resid", resid=xs, gate=m6[5], tk=2048, name="mlp_down")

        fc = front(xc, 1, False)
        fl = front(xl, 0, True)
        dv_c, dv_l = fc["pa"][:, col_dv:col_dv + wd], fl["pa"][:, col_dv:col_dv + wd]
        mv_c, mv_l = fc["kv_raw"][:, wm:], fl["kv_raw"][:, wm:]
        dk_all = jnp.concatenate([fc["dk"], fl["dk"]], axis=0)
        dv_all = _with_ones(jnp.concatenate([dv_c, dv_l], axis=0), hd_n)
        mk_all = jnp.concatenate([fc["mk"], fl["mk"]], axis=0)
        mv_all = _with_ones(jnp.concatenate([mv_c, mv_l], axis=0), hm_n)
        diff_args = (lam_p, sub_g, lam_init)

        a_l = _flash(fl["dq"], dk_all, dv_all, heads=hd_n, dqk=DIFF_VD, diff=diff_args, tq=256, name="diff_attn")
        b_l = _flash(fl["mq"], mk_all, mv_all, heads=hm_n, dqk=MLA_QK_PAD, name="mla_attn")
        n_l = _na_attention(fl["nq"], fl["nk"], fl["pa"], col_nv, fc["nk"], fc["pa"], col_nv, na_bias, heads=hn_n)
        xl = back(xl, fl, a_l, b_l, n_l)
        if need_ctx:
            nv_c = fc["pa"][:, col_nv:col_nv + wn]
            a_c = _flash(fc["dq"], fc["dk"], _with_ones(dv_c, hd_n), heads=hd_n, dqk=DIFF_VD, diff=diff_args,
                         name="diff_attn_ctx")
            b_c = _flash(fc["mq"], fc["mk"], _with_ones(mv_c, hm_n), heads=hm_n, dqk=MLA_QK_PAD, name="mla_attn_ctx")
            n_c = _flash(fc["nq"], fc["nk"], _with_ones(nv_c, hn_n), heads=hn_n, dqk=NA_DH, name="na_attn_ctx")
            xc = back(xc, fc, a_c, b_c, n_c)
    return xl[None]
```

```python
import functools
import math

import numpy as np
import jax
import jax.numpy as jnp
from jax import lax
from jax.experimental import pallas as pl
from jax.experimental.pallas import tpu as pltpu

F32 = jnp.float32
BF16 = jnp.bfloat16

NORM_EPS = 1e-6
ROPE_THETA = 10000.0
GRID_W = 64
DIFF_DH = 64
DIFF_VD = 2 * DIFF_DH
MLA_NOPE = 128
MLA_ROPE = 64
MLA_QK = MLA_NOPE + MLA_ROPE
MLA_QK_PAD = 256
MLA_V = 128
NA_DH = 128
NA_KH = 8
NA_KW = 16
NA_QROWS = 4
NA_WROWS = 12
N_MOD = 6
N_BRANCH = 3
LOG2E = 1.4426950408889634
MASKED = -1e30
LANES = 128
ROW_CHUNK = 32
VMEM_LIMIT_BYTES = 56 * 1024 * 1024


def _params(*semantics):
    return pltpu.CompilerParams(dimension_semantics=semantics, vmem_limit_bytes=VMEM_LIMIT_BYTES)


def _dot(a, b):
    return jnp.dot(a, b, preferred_element_type=F32)


def _dot_nt(a, b):
    return lax.dot_general(a, b, (((1,), (1,)), ((), ())), preferred_element_type=F32)


def _ada_kernel(cond_ref, wd_ref, wu_ref, b_ref, o_ref):
    cnd = cond_ref[...]
    act = cnd / (1.0 + jnp.exp(-cnd))
    low = _dot(act.astype(BF16), wd_ref[...])
    o_ref[0] = _dot(low.astype(BF16), wu_ref[0]) + b_ref[0]


def _ada(cond, w_down, w_up, b):
    rows, d = cond.shape
    depth, rank, n = w_up.shape
    tn = min(n, 4096)
    return pl.pallas_call(
        _ada_kernel,
        grid=(depth, n // tn),
        in_specs=[pl.BlockSpec((rows, d), lambda l, j: (0, 0)),
                  pl.BlockSpec((d, rank), lambda l, j: (0, 0)),
                  pl.BlockSpec((1, rank, tn), lambda l, j: (l, 0, j)),
                  pl.BlockSpec((1, 1, tn), lambda l, j: (l, 0, j))],
        out_specs=pl.BlockSpec((1, rows, tn), lambda l, j: (l, 0, j)),
        out_shape=jax.ShapeDtypeStruct((depth, rows, n), F32),
        compiler_params=_params("arbitrary", "arbitrary"),
        name="ada_mod",
    )(cond, w_down, w_up, b.reshape(depth, 1, n))


def _modulate_kernel(x_ref, g_ref, sh_ref, sc_ref, o_ref):
    x = x_ref[...]
    r = lax.rsqrt(jnp.mean(x * x, axis=-1, keepdims=True) + NORM_EPS)
    y = (x * r) * g_ref[...]
    o_ref[...] = (y * (1.0 + sc_ref[...]) + sh_ref[...]).astype(o_ref.dtype)


def _modulate(x, g, shift, scale, tm=256):
    m, d = x.shape
    tm = min(tm, m)
    vec = pl.BlockSpec((1, d), lambda i: (0, 0))
    return pl.pallas_call(
        _modulate_kernel,
        grid=(m // tm,),
        in_specs=[pl.BlockSpec((tm, d), lambda i: (i, 0)), vec, vec, vec],
        out_specs=pl.BlockSpec((tm, d), lambda i: (i, 0)),
        out_shape=jax.ShapeDtypeStruct((m, d), BF16),
        compiler_params=_params("arbitrary"),
        name="modulate",
    )(x, g, shift, scale)


def _mm_kernel(*refs, nk, n_norm, epilogue):
    it = iter(refs)
    a_ref, w_ref = next(it), next(it)
    g_ref = next(it) if n_norm else None
    x_ref, gate_ref = (next(it), next(it)) if epilogue == "resid" else (None, None)
    o_ref = next(it)
    acc_ref = next(it) if nk > 1 else None

    a = a_ref[...]
    if n_norm:
        af = a.astype(F32)
        if n_norm < af.shape[1]:
            normed = lax.broadcasted_iota(jnp.int32, af.shape, 1) < n_norm
            ssq = jnp.sum(jnp.where(normed, af * af, 0.0), axis=-1, keepdims=True)
            r = lax.rsqrt(ssq * (1.0 / n_norm) + NORM_EPS)
            a = jnp.where(normed, (af * r) * g_ref[...], af).astype(BF16)
        else:
            r = lax.rsqrt(jnp.mean(af * af, axis=-1, keepdims=True) + NORM_EPS)
            a = ((af * r) * g_ref[...]).astype(BF16)
    part = _dot(a, w_ref[...])

    def finish(acc):
        if epilogue == "relu2":
            acc = jnp.maximum(acc, 0.0)
            acc = acc * acc
        elif epilogue == "resid":
            acc = x_ref[...] + gate_ref[...] * acc
        o_ref[...] = acc.astype(o_ref.dtype)

    if nk == 1:
        finish(part)
    else:
        k = pl.program_id(2)

        @pl.when(k == 0)
        def _():
            acc_ref[...] = part

        @pl.when(k > 0)
        def _():
            acc_ref[...] += part

        @pl.when(k == nk - 1)
        def _():
            finish(acc_ref[...])


def _matmul(a, w, *, a_col=0, out_dtype=BF16, norm_gain=None, n_norm=0, epilogue=None,
            resid=None, gate=None, tm=1024, tn=1024, tk=None, name="matmul"):
    m = a.shape[0]
    k_dim, n = w.shape
    tm, tn, tk = min(tm, m), min(tn, n), min(tk or k_dim, k_dim)
    assert m % tm == 0 and n % tn == 0 and k_dim % tk == 0 and a_col % tk == 0, (m, n, k_dim, tm, tn, tk, a_col)
    nk = k_dim // tk
    a_off = a_col // tk
    assert not (n_norm and nk > 1)
    in_specs = [pl.BlockSpec((tm, tk), lambda i, j, k: (i, a_off + k)),
                pl.BlockSpec((tk, tn), lambda i, j, k: (k, j))]
    args = [a, w]
    if n_norm:
        in_specs.append(pl.BlockSpec((1, tk), lambda i, j, k: (0, 0)))
        args.append(norm_gain)
    aliases = {}
    if epilogue == "resid":
        aliases = {len(args): 0}
        in_specs += [pl.BlockSpec((tm, tn), lambda i, j, k: (i, j)),
                     pl.BlockSpec((1, tn), lambda i, j, k: (0, j))]
        args += [resid, gate]
    return pl.pallas_call(
        functools.partial(_mm_kernel, nk=nk, n_norm=n_norm, epilogue=epilogue),
        grid=(m // tm, n // tn, nk),
        in_specs=in_specs,
        out_specs=pl.BlockSpec((tm, tn), lambda i, j, k: (i, j)),
        out_shape=jax.ShapeDtypeStruct((m, n), out_dtype),
        scratch_shapes=[pltpu.VMEM((tm, tn), F32)] if nk > 1 else [],
        input_output_aliases=aliases,
        compiler_params=_params("arbitrary", "arbitrary", "arbitrary"),
        name=name,
    )(*args)


def _headnorm_kernel(*refs, wb, nchunk, inv_cnt, rope):
    if rope:
        x_ref, g_ref, gm_ref, gr_ref, pm_ref, cos_ref, sin_ref, o_ref = refs
    else:
        x_ref, g_ref, gm_ref, o_ref = refs
    for c in range(nchunk):
        cols = slice(c * wb, (c + 1) * wb)
        xb = x_ref[:, cols]
        xf = xb.astype(F32)
        sq = xf * xf
        hi = sq.astype(BF16)
        lo = (sq - hi.astype(F32)).astype(BF16)
        ssq = _dot(hi, gm_ref[...]) + _dot(lo, gm_ref[...])
        r = lax.rsqrt(ssq * inv_cnt + NORM_EPS)
        y = (xf * r) * g_ref[:, cols]
        if rope:
            yr = (_dot(xb, pm_ref[...]) * r) * gr_ref[:, cols]
            y = y * cos_ref[...] + yr * sin_ref[...]
        o_ref[:, cols] = y.astype(o_ref.dtype)


def _headnorm(x, col, width, *, wb, gain, gmat, inv_cnt, rope=None, tm=512, name="headnorm"):
    m = x.shape[0]
    tm = min(tm, m)
    assert m % tm == 0 and col % width == 0 and width % wb == 0
    cb = col // width
    row = pl.BlockSpec((1, width), lambda i: (0, 0))
    mat = pl.BlockSpec((wb, wb), lambda i: (0, 0))
    in_specs = [pl.BlockSpec((tm, width), lambda i: (i, cb)), row, mat]
    args = [x, gain, gmat]
    if rope is not None:
        gain_rot, pmat, cos, sin = rope
        tab = pl.BlockSpec((tm, wb), lambda i: (i, 0))
        in_specs += [row, mat, tab, tab]
        args += [gain_rot, pmat, cos, sin]
    return pl.pallas_call(
        functools.partial(_headnorm_kernel, wb=wb, nchunk=width // wb, inv_cnt=inv_cnt, rope=rope is not None),
        grid=(m // tm,),
        in_specs=in_specs,
        out_specs=pl.BlockSpec((tm, width), lambda i: (i, 0)),
        out_shape=jax.ShapeDtypeStruct((m, width), BF16),
        compiler_params=_params("arbitrary"),
        name=name,
    )(*args)


def _flash_kernel(*refs, diff, tq, tk, n, lam_init):
    if diff:
        q_ref, k_ref, v_ref, lp_ref, sg_ref, o_ref, qs_ref, s0, s1, p0, p1, a0, a1, m_ref, acc_ref = refs
    else:
        q_ref, k_ref, v_ref, o_ref, s0, s1, p0, p1, a0, a1, m_ref, acc_ref = refs
        qs_ref = q_ref
    s_slots, p_slots, a_slots = (s0, s1), (p0, p1), (a0, a1)
    rows = m_ref.shape[0]

    m_ref[...] = jnp.full(m_ref.shape, MASKED, F32)
    acc_ref[...] = jnp.zeros(acc_ref.shape, F32)
    if diff:
        q = q_ref[...].astype(F32)
        first = lax.broadcasted_iota(jnp.int32, q.shape, 1) < DIFF_DH
        qs_ref[0:tq, :] = jnp.where(first, q, 0.0).astype(BF16)
        qs_ref[tq:2 * tq, :] = jnp.where(first, 0.0, q).astype(BF16)

    def key_rows(j):
        off = j * tk
        return pl.ds(off if isinstance(j, int) else pl.multiple_of(off, 2 * LANES), tk)

    def scores(j, slot):
        s_slots[slot][...] = _dot_nt(qs_ref[...], k_ref[key_rows(j), :])

    def softmax(slot):
        s_ref, p_ref, a_ref = s_slots[slot], p_slots[slot], a_slots[slot]
        for r in range(0, rows, ROW_CHUNK):
            rs = slice(r, r + ROW_CHUNK)
            m_prev = m_ref[rs, :]
            m_new = jnp.maximum(m_prev, jnp.max(s_ref[rs, :], axis=1, keepdims=True))
            a_ref[rs, :] = jnp.exp2(m_prev - m_new)
            m_ref[rs, :] = m_new
            p_ref[rs, :] = jnp.exp2(s_ref[rs, :] - m_new[:, :1]).astype(BF16)

    def values(j, slot):
        alpha = a_slots[slot][...]
        acc_ref[...] = (jnp.concatenate([alpha, alpha], axis=1) * acc_ref[...]
                        + _dot(p_slots[slot][...], v_ref[key_rows(j), :]))

    def step(j, parity, do_scores, do_softmax, do_values):
        if do_scores:
            scores(j, parity)
        if do_softmax:
            softmax(1 - parity)
        if do_values:
            values(j - 2, parity)

    pairs = max(0, (n - 2) // 2)
    for j in range(min(2, n + 2)):
        step(j, j % 2, j < n, 1 <= j <= n, False)

    def pair(i, carry):
        j = 2 + 2 * i
        step(j, 0, True, True, True)
        step(j + 1, 1, True, True, True)
        return carry

    if pairs:
        lax.fori_loop(0, pairs, pair, 0)
    for j in range(2 + 2 * pairs, n + 2):
        step(j, j % 2, j < n, j <= n, True)

    acc = acc_ref[...]
    o = acc[:, :LANES] / acc[:, LANES:]
    if diff:
        lp = lp_ref[...]
        lam = (jnp.exp(jnp.sum(lp[0:1] * lp[1:2], axis=1, keepdims=True))
               - jnp.exp(jnp.sum(lp[2:3] * lp[3:4], axis=1, keepdims=True)) + lam_init)
        d = o[0:tq] - lam * o[tq:2 * tq]
        r = lax.rsqrt(jnp.mean(d * d, axis=-1, keepdims=True) + NORM_EPS)
        o = ((d * r) * sg_ref[...]) * (1.0 - lam_init)
    o_ref[...] = o.astype(o_ref.dtype)


def _with_ones(v, heads):
    v3 = v.reshape(v.shape[0], heads, LANES)
    return jnp.concatenate([v3, jnp.ones_like(v3)], axis=2).reshape(v.shape[0], 2 * heads * LANES)


def _flash(q, k, v_ones, *, heads, dqk, k_col=0, diff=None, tq=512, tk=1280, name="flash"):
    m = q.shape[0]
    skv = k.shape[0]
    tq, tk = min(tq, m), min(tk, skv)
    assert m % tq == 0 and skv % tk == 0 and k_col % dqk == 0 and tk % (2 * LANES) == 0
    kb = k_col // dqk
    rows = 2 * tq if diff else tq
    assert rows % ROW_CHUNK == 0
    in_specs = [pl.BlockSpec((tq, dqk), lambda h, i: (i, h)),
                pl.BlockSpec((skv, dqk), lambda h, i: (0, kb + h)),
                pl.BlockSpec((skv, 2 * LANES), lambda h, i: (0, h))]
    args = [q, k, v_ones]
    scratch = []
    lam_init = 0.0
    if diff:
        lam_p, sub_g, lam_init = diff
        in_specs += [pl.BlockSpec(lam_p.shape, lambda h, i: (0, 0)),
                     pl.BlockSpec((1, LANES), lambda h, i: (0, 0))]
        args += [lam_p, sub_g]
        scratch.append(pltpu.VMEM((rows, dqk), BF16))
    scratch += [pltpu.VMEM((rows, tk), F32), pltpu.VMEM((rows, tk), F32),
                pltpu.VMEM((rows, tk), BF16), pltpu.VMEM((rows, tk), BF16),
                pltpu.VMEM((rows, LANES), F32), pltpu.VMEM((rows, LANES), F32),
                pltpu.VMEM((rows, LANES), F32), pltpu.VMEM((rows, 2 * LANES), F32)]
    return pl.pallas_call(
        functools.partial(_flash_kernel, diff=bool(diff), tq=tq, tk=tk, n=skv // tk, lam_init=lam_init),
        grid=(heads, m // tq),
        in_specs=in_specs,
        out_specs=pl.BlockSpec((tq, LANES), lambda h, i: (i, h)),
        out_shape=jax.ShapeDtypeStruct((m, heads * LANES), BF16),
        scratch_shapes=scratch,
        compiler_params=_params("arbitrary", "arbitrary"),
        name=name,
    )(*args)


def _na_kernel(q_ref, k_ref, v_ref, kc_ref, vc_ref, b_ref, o_ref, *, nb, rows):
    step = pl.program_id(1)
    kc = kc_ref[...]
    vc = vc_ref[...]
    tq, win = NA_QROWS * GRID_W, NA_WROWS * GRID_W
    last = rows // NA_QROWS - 1

    def one_block(bb, carry):
        b = step * nb + bb
        first_row = jnp.clip(b * NA_QROWS - NA_KH // 2, 0, rows - NA_WROWS)
        case = jnp.where(b == 0, 0, jnp.where(b == last, 2, 1))
        q_rows = pl.ds(pl.multiple_of(bb * tq, tq), tq)
        k_rows = pl.ds(pl.multiple_of(first_row * GRID_W, tq), win)
        q = q_ref[q_rows, :]
        s_win = _dot_nt(q, k_ref[k_rows, :]) + b_ref[0, case]
        s_ctx = _dot_nt(q, kc)
        m = jnp.maximum(jnp.max(s_win, axis=1, keepdims=True), jnp.max(s_ctx, axis=1, keepdims=True))
        p_win = jnp.exp2(s_win - m)
        p_ctx = jnp.exp2(s_ctx - m)
        l = jnp.sum(p_win, axis=1, keepdims=True) + jnp.sum(p_ctx, axis=1, keepdims=True)
        o = _dot(p_win.astype(BF16), v_ref[k_rows, :]) + _dot(p_ctx.astype(BF16), vc)
        o_ref[q_rows, :] = (o / l).astype(o_ref.dtype)
        return carry

    lax.fori_loop(0, nb, one_block, 0)


def _na_attention(q, k, v, v_col, kc, vc, vc_col, bias, *, heads, nb=4):
    s = q.shape[0]
    rows = s // GRID_W
    nb = min(nb, rows // NA_QROWS)
    assert s % GRID_W == 0 and rows >= NA_WROWS and rows % (NA_QROWS * nb) == 0
    nctx = kc.shape[0]
    vb, vcb = v_col // NA_DH, vc_col // NA_DH
    tq = nb * NA_QROWS * GRID_W
    return pl.pallas_call(
        functools.partial(_na_kernel, nb=nb, rows=rows),
        grid=(heads, s // tq),
        in_specs=[pl.BlockSpec((tq, NA_DH), lambda h, i: (i, h)),
                  pl.BlockSpec((s, NA_DH), lambda h, i: (0, h)),
                  pl.BlockSpec((s, NA_DH), lambda h, i: (0, vb + h)),
                  pl.BlockSpec((nctx, NA_DH), lambda h, i: (0, h)),
                  pl.BlockSpec((nctx, NA_DH), lambda h, i: (0, vcb + h)),
                  pl.BlockSpec((1,) + bias.shape[1:], lambda h, i: (h, 0, 0, 0))],
        out_specs=pl.BlockSpec((tq, NA_DH), lambda h, i: (i, h)),
        out_shape=jax.ShapeDtypeStruct((s, heads * NA_DH), BF16),
        compiler_params=_params("arbitrary", "arbitrary"),
        name="na_attention",
    )(q, k, v, kc, vc, bias)


def _merge_kernel(a_ref, b_ref, n_ref, hd_ref, wpa_ref, wpb_ref, wpc_ref, wga_ref, wgb_ref, wgc_ref,
                  bga_ref, bgb_ref, bgc_ref, o_ref):
    hd = hd_ref[...]

    def gated(o_ref_, wp_ref, wg_ref, bg_ref):
        z = _dot(hd, wg_ref[...]) + bg_ref[...]
        return (1.0 / (1.0 + jnp.exp(-z))) * _dot(o_ref_[...], wp_ref[...])

    out = gated(a_ref, wpa_ref, wga_ref, bga_ref) + gated(b_ref, wpb_ref, wgb_ref, bgb_ref)
    out = out + gated(n_ref, wpc_ref, wgc_ref, bgc_ref)
    o_ref[...] = out.astype(o_ref.dtype)


def _merge(a, b, n, hd, hd_col, wpa, wpb, wpc, wgu, bg, tm=1024, tn=512):
    m = a.shape[0]
    d = wpa.shape[1]
    rank = wgu.shape[0]
    tm, tn = min(tm, m), min(tn, d)
    assert m % tm == 0 and d % tn == 0 and hd_col % rank == 0
    nj = d // tn
    hb = hd_col // rank

    def act(w):
        return pl.BlockSpec((tm, w), lambda i, j: (i, 0))

    def wt(rows, off):
        return pl.BlockSpec((rows, tn), lambda i, j: (0, off * nj + j))

    return pl.pallas_call(
        _merge_kernel,
        grid=(m // tm, nj),
        in_specs=[act(a.shape[1]), act(b.shape[1]), act(n.shape[1]),
                  pl.BlockSpec((tm, rank), lambda i, j: (i, hb)),
                  wt(wpa.shape[0], 0), wt(wpb.shape[0], 0), wt(wpc.shape[0], 0),
                  wt(rank, 0), wt(rank, 1), wt(rank, 2), wt(1, 0), wt(1, 1), wt(1, 2)],
        out_specs=pl.BlockSpec((tm, tn), lambda i, j: (i, j)),
        out_shape=jax.ShapeDtypeStruct((m, d), BF16),
        compiler_params=_params("arbitrary", "arbitrary"),
        name="branch_merge",
    )(a, b, n, hd, wpa, wpb, wpc, wgu, wgu, wgu, bg, bg, bg)


def _rope_pattern(n_tokens):
    t = np.arange(n_tokens)
    row = jnp.asarray(t // GRID_W, F32)
    col = jnp.asarray(t % GRID_W, F32)
    axis_dim = MLA_ROPE // 2
    inv = ROPE_THETA ** (-jnp.arange(0, axis_dim, 2, dtype=F32) / axis_dim)
    ang_r, ang_c = row[:, None] * inv, col[:, None] * inv
    ang = jnp.concatenate([ang_r, ang_r, ang_c, ang_c], axis=1)
    return jnp.cos(ang), jnp.sin(ang)


def _rotate_half_matrix(width, lanes):
    p = np.zeros((width, width), np.float32)
    q = MLA_ROPE // 4
    for i in lanes:
        if (i % (2 * q)) < q:
            p[i + q, i] = -1.0
        else:
            p[i - q, i] = 1.0
    return p


def _group_matrix(width, group):
    idx = np.arange(width) // group
    return (idx[:, None] == idx[None, :]).astype(np.float32)


def _na_bias_table(rpb):
    heads = rpb.shape[0]
    edge = GRID_W - NA_KW
    rp = jnp.pad(rpb.astype(F32), ((0, 0), (NA_WROWS - 1, NA_WROWS - 1), (edge, edge)))
    by_col = jnp.stack([rp[:, :, GRID_W - 1 - q:2 * GRID_W - 1 - q] for q in range(GRID_W)], axis=2)
    qc = np.arange(GRID_W)[:, None]
    kcol = np.arange(GRID_W)[None, :]
    start = np.clip(qc - NA_KW // 2, 0, edge)
    valid_col = (kcol >= start) & (kcol < start + NA_KW)
    jr = np.arange(NA_WROWS)
    half = NA_KH // 2
    cases = [[(i, 0) for i in range(NA_QROWS)],
             [(half + i, i) for i in range(NA_QROWS)],
             [(NA_KH + i, NA_WROWS - NA_KH) for i in range(NA_QROWS)]]
    tables = []
    for case in cases:
        per_row = []
        for dq, dr0 in case:
            lo = NA_WROWS - 1 + NA_KH - 1 - dq
            valid = ((jr >= dr0) & (jr < dr0 + NA_KH))[:, None, None] & valid_col[None]
            tile = jnp.where(valid[None], by_col[:, lo:lo + NA_WROWS] * LOG2E, MASKED)
            per_row.append(tile.transpose(0, 2, 1, 3).reshape(heads, GRID_W, NA_WROWS * GRID_W))
        tables.append(jnp.concatenate(per_row, axis=1))
    return jnp.stack(tables, axis=1)


def kernel(x, c, ctx, c_ctx, w_ada_down, w_ada_up, b_ada, norm_attn_g, norm_mlp_g, w_in, diff_lambda, diff_q_norm, diff_k_norm, diff_subln, mla_q_a_norm, mla_kv_a_norm, w_mla_uq, w_mla_ukv, mla_q_norm, mla_k_norm, na_q_norm, na_k_norm, na_rpb, w_proj_a, w_proj_b, w_proj_c, w_gate_down, w_gate_up, b_gate, w_out, w_mlp_up, w_mlp_down):
    batch, seq, d = x.shape
    assert batch == 1
    depth = w_in.shape[0]
    hd_n = w_proj_a.shape[1] // DIFF_VD
    hm_n = w_proj_b.shape[1] // MLA_V
    hn_n = w_proj_c.shape[1] // NA_DH
    q_rank = mla_q_a_norm.shape[1]
    kv_rank = mla_kv_a_norm.shape[1]
    g_rank = w_gate_down.shape[2]
    wd, wm, wn = hd_n * DIFF_VD, hm_n * MLA_QK_PAD, hn_n * NA_DH
    kv_in = kv_rank + 2 * MLA_ROPE

    col_dq, col_dk, col_dv, col_cq = 0, wd, 2 * wd, 3 * wd
    col_nq = col_cq + q_rank
    col_nk, col_nv = col_nq + wn, col_nq + 2 * wn
    col_gd, col_kv = 0, g_rank + LANES

    cos64, sin64 = _rope_pattern(seq)
    ones64 = jnp.ones((seq, MLA_ROPE), F32)
    zeros64 = jnp.zeros((seq, MLA_ROPE), F32)
    cos_d, sin_d = jnp.tile(cos64, (1, 2)), jnp.tile(sin64, (1, 2))
    cos_m = jnp.concatenate([ones64, ones64, cos64, ones64], axis=1)
    sin_m = jnp.concatenate([zeros64, zeros64, sin64, zeros64], axis=1)
    pm_d = jnp.asarray(_rotate_half_matrix(LANES, range(LANES)), BF16)
    pm_m = jnp.asarray(_rotate_half_matrix(MLA_QK_PAD, range(MLA_NOPE, MLA_QK)), BF16)
    gm_d = jnp.asarray(_group_matrix(LANES, DIFF_DH), BF16)
    gm_m = jnp.asarray(_group_matrix(MLA_QK_PAD, MLA_QK_PAD), BF16)
    gm_n = jnp.asarray(_group_matrix(LANES, NA_DH), BF16)
    rot_d = np.abs(_rotate_half_matrix(LANES, range(LANES))).argmax(axis=0)

    cond = jnp.zeros((16, d), F32).at[0].set(c[0]).at[1].set(c_ctx)
    mod = _ada(cond, w_ada_down.astype(BF16), w_ada_up.astype(BF16), b_ada)

    def gain_row(g, reps, scale=1.0):
        return jnp.tile(g.astype(F32) * scale, reps)[None, :]

    def pad_heads(g):
        return jnp.concatenate([g.astype(F32), jnp.zeros((MLA_QK_PAD - MLA_QK,), F32)])

    xl, xc = x[0], ctx[0]
    for l in range(depth):
        need_ctx = l < depth - 1
        lam_init = 0.8 - 0.6 * math.exp(-0.3 * l)

        wi = w_in[l]
        sp = np.cumsum([0, wd, wd, wd, q_rank, kv_rank, MLA_ROPE, wn, wn, wn])
        part = [wi[:, sp[i]:sp[i + 1]] for i in range(9)]
        head_major = lambda w: w.reshape(d, 2, hd_n, DIFF_DH).transpose(0, 2, 1, 3).reshape(d, wd)
        w1a = jnp.concatenate([head_major(part[0]), head_major(part[1]), part[2], part[3],
                               part[6], part[7], part[8]], axis=1).astype(BF16)
        w1b = jnp.concatenate([w_gate_down[l], jnp.zeros((d, LANES), F32), part[4], part[5],
                               jnp.zeros((d, MLA_ROPE), F32)], axis=1).astype(BF16)
        uq = w_mla_uq[l].reshape(q_rank, hm_n, MLA_QK)
        w_uq = jnp.concatenate([uq, jnp.zeros((q_rank, hm_n, MLA_QK_PAD - MLA_QK), F32)],
                               axis=2).reshape(q_rank, wm).astype(BF16)
        ukv = w_mla_ukv[l].reshape(kv_rank, hm_n, MLA_NOPE + MLA_V)
        k_rows = jnp.concatenate([ukv[:, :, :MLA_NOPE], jnp.zeros((kv_rank, hm_n, MLA_QK_PAD - MLA_NOPE), F32)], axis=2)
        pe_rows = jnp.concatenate([jnp.zeros((MLA_ROPE, hm_n, MLA_NOPE), F32),
                                   jnp.broadcast_to(jnp.eye(MLA_ROPE, dtype=F32)[:, None, :], (MLA_ROPE, hm_n, MLA_ROPE)),
                                   jnp.zeros((MLA_ROPE, hm_n, MLA_QK_PAD - MLA_QK), F32)], axis=2)
        w_kv = jnp.concatenate([
            jnp.concatenate([k_rows.reshape(kv_rank, wm), ukv[:, :, MLA_NOPE:].reshape(kv_rank, hm_n * MLA_V)], axis=1),
            jnp.concatenate([pe_rows.reshape(MLA_ROPE, wm), jnp.zeros((MLA_ROPE, hm_n * MLA_V), F32)], axis=1),
            jnp.zeros((MLA_ROPE, wm + hm_n * MLA_V), F32)], axis=0).astype(BF16)
        kv_gain = jnp.concatenate([mla_kv_a_norm[l].astype(F32), jnp.ones((2 * MLA_ROPE,), F32)])[None, :]
        wpa, wpb, wpc = w_proj_a[l].astype(BF16), w_proj_b[l].astype(BF16), w_proj_c[l].astype(BF16)
        wgu, bg = w_gate_up[l].astype(BF16), b_gate[l][None, :]
        wo, wup, wdn = w_out[l].astype(BF16), w_mlp_up[l].astype(BF16), w_mlp_down[l].astype(BF16)

        dq_gain = gain_row(diff_q_norm[l], 2, DIFF_DH ** -0.5 * LOG2E)
        dk_gain = gain_row(diff_k_norm[l], 2)
        mq_gain = pad_heads(mla_q_norm[l])[None, :] * (MLA_QK ** -0.5 * LOG2E)
        mk_gain = pad_heads(mla_k_norm[l])[None, :]
        nq_gain = gain_row(na_q_norm[l], 1, NA_DH ** -0.5 * LOG2E)
        nk_gain = gain_row(na_k_norm[l], 1)
        rot_m = np.arange(MLA_QK_PAD)
        rot_m[MLA_NOPE:MLA_QK] = MLA_NOPE + rot_d[:MLA_ROPE]
        na_bias = _na_bias_table(na_rpb[l])
        lam_p = diff_lambda[l].astype(F32)
        sub_g = diff_subln[l].astype(F32)[None, :]

        def front(xs, mrow, rotate):
            m6 = [mod[l, mrow, i * d:(i + 1) * d][None, :] for i in range(N_MOD)]
            h = _modulate(xs, norm_attn_g[l][None, :], m6[0], m6[1])
            pa = _matmul(h, w1a, name="in_proj_a")
            pb = _matmul(h, w1b, tn=w1b.shape[1], name="in_proj_b")

            def rope_args(gain, rot_idx, pmat, cos, sin):
                return (gain[:, rot_idx], pmat, cos, sin) if rotate else None

            tile_d = lambda g: jnp.tile(g, (1, hd_n))
            dq = _headnorm(pa, col_dq, wd, wb=LANES, gain=tile_d(dq_gain), gmat=gm_d, inv_cnt=1.0 / DIFF_DH,
                           rope=rope_args(tile_d(dq_gain), np.tile(rot_d, hd_n) + np.repeat(np.arange(hd_n) * LANES, LANES), pm_d, cos_d, sin_d),
                           name="diff_q_prep")
            dk = _headnorm(pa, col_dk, wd, wb=LANES, gain=tile_d(dk_gain), gmat=gm_d, inv_cnt=1.0 / DIFF_DH,
                           rope=rope_args(tile_d(dk_gain), np.tile(rot_d, hd_n) + np.repeat(np.arange(hd_n) * LANES, LANES), pm_d, cos_d, sin_d),
                           name="diff_k_prep")
            q_raw = _matmul(pa, w_uq, a_col=col_cq, norm_gain=mla_q_a_norm[l].astype(F32)[None, :], n_norm=q_rank,
                            name="mla_q_up")
            kv_raw = _matmul(pb, w_kv, a_col=col_kv, norm_gain=kv_gain, n_norm=kv_rank, name="mla_kv_up")
            tile_m = lambda g: jnp.tile(g, (1, hm_n))
            rot_m_all = np.tile(rot_m, hm_n) + np.repeat(np.arange(hm_n) * MLA_QK_PAD, MLA_QK_PAD)
            mq = _headnorm(q_raw, 0, wm, wb=MLA_QK_PAD, gain=tile_m(mq_gain), gmat=gm_m, inv_cnt=1.0 / MLA_QK,
                           rope=rope_args(tile_m(mq_gain), rot_m_all, pm_m, cos_m, sin_m), tm=256, name="mla_q_prep")
            mk = _headnorm(kv_raw, 0, wm, wb=MLA_QK_PAD, gain=tile_m(mk_gain), gmat=gm_m, inv_cnt=1.0 / MLA_QK,
                           rope=rope_args(tile_m(mk_gain), rot_m_all, pm_m, cos_m, sin_m), tm=256, name="mla_k_prep")
            tile_n = lambda g: jnp.tile(g, (1, hn_n))
            nq = _headnorm(pa, col_nq, wn, wb=LANES, gain=tile_n(nq_gain), gmat=gm_n, inv_cnt=1.0 / NA_DH, name="na_q_prep")
            nk = _headnorm(pa, col_nk, wn, wb=LANES, gain=tile_n(nk_gain), gmat=gm_n, inv_cnt=1.0 / NA_DH, name="na_k_prep")
            return dict(m6=m6, h=h, pa=pa, pb=pb, dq=dq, dk=dk, mq=mq, mk=mk, kv_raw=kv_raw, nq=nq, nk=nk)

        def back(xs, f, o_a, o_b, o_c):
            m6 = f["m6"]
            merged = _merge(o_a, o_b, o_c, f["pb"], col_gd, wpa, wpb, wpc, wgu, bg)
            xs = _matmul(merged, wo, out_dtype=F32, epilogue="resid", resid=xs, gate=m6[2], tn=512, name="out_proj")
            h2 = _modulate(xs, norm_mlp_g[l][None, :], m6[3], m6[4])
            hid = _matmul(h2, wup, epilogue="relu2", name="mlp_up")
            return _matmul(hid, wdn, out_dtype=F32, epilogue="resid", resid=xs, gate=m6[5], tk=2048, name="mlp_down")

        fc = front(xc, 1, False)
        fl = front(xl, 0, True)
        dv_c, dv_l = fc["pa"][:, col_dv:col_dv + wd], fl["pa"][:, col_dv:col_dv + wd]
        mv_c, mv_l = fc["kv_raw"][:, wm:], fl["kv_raw"][:, wm:]
        dk_all = jnp.concatenate([fc["dk"], fl["dk"]], axis=0)
        dv_all = _with_ones(jnp.concatenate([dv_c, dv_l], axis=0), hd_n)
        mk_all = jnp.concatenate([fc["mk"], fl["mk"]], axis=0)
        mv_all = _with_ones(jnp.concatenate([mv_c, mv_l], axis=0), hm_n)
        diff_args = (lam_p, sub_g, lam_init)

        a_l = _flash(fl["dq"], dk_all, dv_all, heads=hd_n, dqk=DIFF_VD, diff=diff_args, tq=256, name="diff_attn")
        b_l = _flash(fl["mq"], mk_all, mv_all, heads=hm_n, dqk=MLA_QK_PAD, name="mla_attn")
        n_l = _na_attention(fl["nq"], fl["nk"], fl["pa"], col_nv, fc["nk"], fc["pa"], col_nv, na_bias, heads=hn_n)
        xl = back(xl, fl, a_l, b_l, n_l)
        if need_ctx:
            nv_c = fc["pa"][:, col_nv:col_nv + wn]
            a_c = _flash(fc["dq"], fc["dk"], _with_ones(dv_c, hd_n), heads=hd_n, dqk=DIFF_VD, diff=diff_args,
                         name="diff_attn_ctx")
            b_c = _flash(fc["mq"], fc["mk"], _with_ones(mv_c, hm_n), heads=hm_n, dqk=MLA_QK_PAD, name="mla_attn_ctx")
            n_c = _flash(fc["nq"], fc["nk"], _with_ones(nv_c, hn_n), heads=hn_n, dqk=NA_DH, name="na_attn_ctx")
            xc = back(xc, fc, a_c, b_c, n_c)
    return xl[None]
```

```python
import functools
import math

import numpy as np
import jax
import jax.numpy as jnp
from jax import lax
from jax.experimental import pallas as pl
from jax.experimental.pallas import tpu as pltpu

F32 = jnp.float32
BF16 = jnp.bfloat16

NORM_EPS = 1e-6
ROPE_THETA = 10000.0
GRID_W = 64
DIFF_DH = 64
DIFF_VD = 2 * DIFF_DH
MLA_NOPE = 128
MLA_ROPE = 64
MLA_QK = MLA_NOPE + MLA_ROPE
MLA_QK_PAD = 256
MLA_V = 128
NA_DH = 128
NA_KH = 8
NA_KW = 16
NA_QROWS = 4
NA_WROWS = 12
N_MOD = 6
N_BRANCH = 3
LOG2E = 1.4426950408889634
MASKED = -1e30
LANES = 128
ROW_CHUNK = 32
VMEM_LIMIT_BYTES = 56 * 1024 * 1024


def _params(*semantics):
    return pltpu.CompilerParams(dimension_semantics=semantics, vmem_limit_bytes=VMEM_LIMIT_BYTES)


def _dot(a, b):
    return jnp.dot(a, b, preferred_element_type=F32)


def _dot_nt(a, b):
    return lax.dot_general(a, b, (((1,), (1,)), ((), ())), preferred_element_type=F32)


def _ada_kernel(cond_ref, wd_ref, wu_ref, b_ref, o_ref):
    cnd = cond_ref[...]
    act = cnd / (1.0 + jnp.exp(-cnd))
    low = _dot(act.astype(BF16), wd_ref[...])
    o_ref[0] = _dot(low.astype(BF16), wu_ref[0]) + b_ref[0]


def _ada(cond, w_down, w_up, b):
    rows, d = cond.shape
    depth, rank, n = w_up.shape
    tn = min(n, 4096)
    return pl.pallas_call(
        _ada_kernel,
        grid=(depth, n // tn),
        in_specs=[pl.BlockSpec((rows, d), lambda l, j: (0, 0)),
                  pl.BlockSpec((d, rank), lambda l, j: (0, 0)),
                  pl.BlockSpec((1, rank, tn), lambda l, j: (l, 0, j)),
                  pl.BlockSpec((1, 1, tn), lambda l, j: (l, 0, j))],
        out_specs=pl.BlockSpec((1, rows, tn), lambda l, j: (l, 0, j)),
        out_shape=jax.ShapeDtypeStruct((depth, rows, n), F32),
        compiler_params=_params("arbitrary", "arbitrary"),
        name="ada_mod",
    )(cond, w_down, w_up, b.reshape(depth, 1, n))


def _modulate_kernel(x_ref, g_ref, sh_ref, sc_ref, o_ref):
    x = x_ref[...]
    r = lax.rsqrt(jnp.mean(x * x, axis=-1, keepdims=True) + NORM_EPS)
    y = (x * r) * g_ref[...]
    o_ref[...] = (y * (1.0 + sc_ref[...]) + sh_ref[...]).astype(o_ref.dtype)


def _modulate(x, g, shift, scale, tm=256):
    m, d = x.shape
    tm = min(tm, m)
    vec = pl.BlockSpec((1, d), lambda i: (0, 0))
    return pl.pallas_call(
        _modulate_kernel,
        grid=(m // tm,),
        in_specs=[pl.BlockSpec((tm, d), lambda i: (i, 0)), vec, vec, vec],
        out_specs=pl.BlockSpec((tm, d), lambda i: (i, 0)),
        out_shape=jax.ShapeDtypeStruct((m, d), BF16),
        compiler_params=_params("arbitrary"),
        name="modulate",
    )(x, g, shift, scale)


def _mm_kernel(*refs, nk, n_norm, epilogue):
    it = iter(refs)
    a_ref, w_ref = next(it), next(it)
    g_ref = next(it) if n_norm else None
    x_ref, gate_ref = (next(it), next(it)) if epilogue == "resid" else (None, None)
    o_ref = next(it)
    acc_ref = next(it) if nk > 1 else None

    a = a_ref[...]
    if n_norm:
        af = a.astype(F32)
        if n_norm < af.shape[1]:
            normed = lax.broadcasted_iota(jnp.int32, af.shape, 1) < n_norm
            ssq = jnp.sum(jnp.where(normed, af * af, 0.0), axis=-1, keepdims=True)
            r = lax.rsqrt(ssq * (1.0 / n_norm) + NORM_EPS)
            a = jnp.where(normed, (af * r) * g_ref[...], af).astype(BF16)
        else:
            r = lax.rsqrt(jnp.mean(af * af, axis=-1, keepdims=True) + NORM_EPS)
            a = ((af * r) * g_ref[...]).astype(BF16)
    part = _dot(a, w_ref[...])

    def finish(acc):
        if epilogue == "relu2":
            acc = jnp.maximum(acc, 0.0)
            acc = acc * acc
        elif epilogue == "resid":
            acc = x_ref[...] + gate_ref[...] * acc
        o_ref[...] = acc.astype(o_ref.dtype)

    if nk == 1:
        finish(part)
    else:
        k = pl.program_id(2)

        @pl.when(k == 0)
        def _():
            acc_ref[...] = part

        @pl.when(k > 0)
        def _():
            acc_ref[...] += part

        @pl.when(k == nk - 1)
        def _():
            finish(acc_ref[...])


def _matmul(a, w, *, a_col=0, out_dtype=BF16, norm_gain=None, n_norm=0, epilogue=None,
            resid=None, gate=None, tm=1024, tn=1024, tk=None, name="matmul"):
    m = a.shape[0]
    k_dim, n = w.shape
    tm, tn, tk = min(tm, m), min(tn, n), min(tk or k_dim, k_dim)
    assert m % tm == 0 and n % tn == 0 and k_dim % tk == 0 and a_col % tk == 0, (m, n, k_dim, tm, tn, tk, a_col)
    nk = k_dim // tk
    a_off = a_col // tk
    assert not (n_norm and nk > 1)
    in_specs = [pl.BlockSpec((tm, tk), lambda i, j, k: (i, a_off + k)),
                pl.BlockSpec((tk, tn), lambda i, j, k: (k, j))]
    args = [a, w]
    if n_norm:
        in_specs.append(pl.BlockSpec((1, tk), lambda i, j, k: (0, 0)))
        args.append(norm_gain)
    aliases = {}
    if epilogue == "resid":
        aliases = {len(args): 0}
        in_specs += [pl.BlockSpec((tm, tn), lambda i, j, k: (i, j)),
                     pl.BlockSpec((1, tn), lambda i, j, k: (0, j))]
        args += [resid, gate]
    return pl.pallas_call(
        functools.partial(_mm_kernel, nk=nk, n_norm=n_norm, epilogue=epilogue),
        grid=(m // tm, n // tn, nk),
        in_specs=in_specs,
        out_specs=pl.BlockSpec((tm, tn), lambda i, j, k: (i, j)),
        out_shape=jax.ShapeDtypeStruct((m, n), out_dtype),
        scratch_shapes=[pltpu.VMEM((tm, tn), F32)] if nk > 1 else [],
        input_output_aliases=aliases,
        compiler_params=_params("arbitrary", "arbitrary", "arbitrary"),
        name=name,
    )(*args)


def _headnorm_kernel(*refs, wb, nchunk, inv_cnt, rope):
    if rope:
        x_ref, g_ref, gm_ref, gr_ref, pm_ref, cos_ref, sin_ref, o_ref = refs
    else:
        x_ref, g_ref, gm_ref, o_ref = refs
    for c in range(nchunk):
        cols = slice(c * wb, (c + 1) * wb)
        xb = x_ref[:, cols]
        xf = xb.astype(F32)
        sq = xf * xf
        hi = sq.astype(BF16)
        lo = (sq - hi.astype(F32)).astype(BF16)
        ssq = _dot(hi, gm_ref[...]) + _dot(lo, gm_ref[...])
        r = lax.rsqrt(ssq * inv_cnt + NORM_EPS)
        y = (xf * r) * g_ref[:, cols]
        if rope:
            yr = (_dot(xb, pm_ref[...]) * r) * gr_ref[:, cols]
            y = y * cos_ref[...] + yr * sin_ref[...]
        o_ref[:, cols] = y.astype(o_ref.dtype)


def _headnorm(x, col, width, *, wb, gain, gmat, inv_cnt, rope=None, tm=512, name="headnorm"):
    m = x.shape[0]
    tm = min(tm, m)
    assert m % tm == 0 and col % width == 0 and width % wb == 0
    cb = col // width
    row = pl.BlockSpec((1, width), lambda i: (0, 0))
    mat = pl.BlockSpec((wb, wb), lambda i: (0, 0))
    in_specs = [pl.BlockSpec((tm, width), lambda i: (i, cb)), row, mat]
    args = [x, gain, gmat]
    if rope is not None:
        gain_rot, pmat, cos, sin = rope
        tab = pl.BlockSpec((tm, wb), lambda i: (i, 0))
        in_specs += [row, mat, tab, tab]
        args += [gain_rot, pmat, cos, sin]
    return pl.pallas_call(
        functools.partial(_headnorm_kernel, wb=wb, nchunk=width // wb, inv_cnt=inv_cnt, rope=rope is not None),
        grid=(m // tm,),
        in_specs=in_specs,
        out_specs=pl.BlockSpec((tm, width), lambda i: (i, 0)),
        out_shape=jax.ShapeDtypeStruct((m, width), BF16),
        compiler_params=_params("arbitrary"),
        name=name,
    )(*args)


def _flash_kernel(*refs, diff, tq, tk, n, lam_init):
    if diff:
        q_ref, k_ref, v_ref, lp_ref, sg_ref, o_ref, qs_ref, s0, s1, p0, p1, a0, a1, m_ref, acc_ref = refs
    else:
        q_ref, k_ref, v_ref, o_ref, s0, s1, p0, p1, a0, a1, m_ref, acc_ref = refs
        qs_ref = q_ref
    s_slots, p_slots, a_slots = (s0, s1), (p0, p1), (a0, a1)
    rows = m_ref.shape[0]

    m_ref[...] = jnp.full(m_ref.shape, MASKED, F32)
    acc_ref[...] = jnp.zeros(acc_ref.shape, F32)
    if diff:
        q = q_ref[...].astype(F32)
        first = lax.broadcasted_iota(jnp.int32, q.shape, 1) < DIFF_DH
        qs_ref[0:tq, :] = jnp.where(first, q, 0.0).astype(BF16)
        qs_ref[tq:2 * tq, :] = jnp.where(first, 0.0, q).astype(BF16)

    def key_rows(j):
        off = j * tk
        return pl.ds(off if isinstance(j, int) else pl.multiple_of(off, 2 * LANES), tk)

    def scores(j, slot):
        s_slots[slot][...] = _dot(qs_ref[...], k_ref[0, j])

    def softmax(slot):
        s_ref, p_ref, a_ref = s_slots[slot], p_slots[slot], a_slots[slot]
        for r in range(0, rows, ROW_CHUNK):
            rs = slice(r, r + ROW_CHUNK)
            m_prev = m_ref[rs, :]
            m_new = jnp.maximum(m_prev, jnp.max(s_ref[rs, :], axis=1, keepdims=True))
            a_ref[rs, :] = jnp.exp2(m_prev - m_new)
            m_ref[rs, :] = m_new
            p_ref[rs, :] = jnp.exp2(s_ref[rs, :] - m_new[:, :1]).astype(BF16)

    def values(j, slot):
        alpha = a_slots[slot][...]
        acc_ref[...] = (jnp.concatenate([alpha, alpha], axis=1) * acc_ref[...]
                        + _dot(p_slots[slot][...], v_ref[key_rows(j), :]))

    def step(j, parity, do_scores, do_softmax, do_values):
        if do_scores:
            scores(j, parity)
        if do_softmax:
            softmax(1 - parity)
        if do_values:
            values(j - 2, parity)

    pairs = max(0, (n - 2) // 2)
    for j in range(min(2, n + 2)):
        step(j, j % 2, j < n, 1 <= j <= n, False)

    def pair(i, carry):
        j = 2 + 2 * i
        step(j, 0, True, True, True)
        step(j + 1, 1, True, True, True)
        return carry

    if pairs:
        lax.fori_loop(0, pairs, pair, 0)
    for j in range(2 + 2 * pairs, n + 2):
        step(j, j % 2, j < n, j <= n, True)

    acc = acc_ref[...]
    o = acc[:, :LANES] / acc[:, LANES:]
    if diff:
        lp = lp_ref[...]
        lam = (jnp.exp(jnp.sum(lp[0:1] * lp[1:2], axis=1, keepdims=True))
               - jnp.exp(jnp.sum(lp[2:3] * lp[3:4], axis=1, keepdims=True)) + lam_init)
        d = o[0:tq] - lam * o[tq:2 * tq]
        r = lax.rsqrt(jnp.mean(d * d, axis=-1, keepdims=True) + NORM_EPS)
        o = ((d * r) * sg_ref[...]) * (1.0 - lam_init)
    o_ref[...] = o.astype(o_ref.dtype)


def _with_ones(v, heads):
    v3 = v.reshape(v.shape[0], heads, LANES)
    return jnp.concatenate([v3, jnp.ones_like(v3)], axis=2).reshape(v.shape[0], 2 * heads * LANES)


def _key_chunks(k, heads, tk):
    skv, width = k.shape
    return k.reshape(skv // tk, tk, heads, width // heads).transpose(2, 0, 3, 1)


def _flash(q, k, v_ones, *, heads, dqk, diff=None, tq=512, tk=1280, name="flash"):
    m = q.shape[0]
    skv = k.shape[0]
    tq, tk = min(tq, m), min(tk, skv)
    assert m % tq == 0 and skv % tk == 0 and k.shape[1] == heads * dqk and tk % (2 * LANES) == 0
    n = skv // tk
    rows = 2 * tq if diff else tq
    assert rows % ROW_CHUNK == 0
    in_specs = [pl.BlockSpec((tq, dqk), lambda h, i: (i, h)),
                pl.BlockSpec((1, n, dqk, tk), lambda h, i: (h, 0, 0, 0)),
                pl.BlockSpec((skv, 2 * LANES), lambda h, i: (0, h))]
    args = [q, _key_chunks(k, heads, tk), v_ones]
    scratch = []
    lam_init = 0.0
    if diff:
        lam_p, sub_g, lam_init = diff
        in_specs += [pl.BlockSpec(lam_p.shape, lambda h, i: (0, 0)),
                     pl.BlockSpec((1, LANES), lambda h, i: (0, 0))]
        args += [lam_p, sub_g]
        scratch.append(pltpu.VMEM((rows, dqk), BF16))
    scratch += [pltpu.VMEM((rows, tk), F32), pltpu.VMEM((rows, tk), F32),
                pltpu.VMEM((rows, tk), BF16), pltpu.VMEM((rows, tk), BF16),
                pltpu.VMEM((rows, LANES), F32), pltpu.VMEM((rows, LANES), F32),
                pltpu.VMEM((rows, LANES), F32), pltpu.VMEM((rows, 2 * LANES), F32)]
    return pl.pallas_call(
        functools.partial(_flash_kernel, diff=bool(diff), tq=tq, tk=tk, n=skv // tk, lam_init=lam_init),
        grid=(heads, m // tq),
        in_specs=in_specs,
        out_specs=pl.BlockSpec((tq, LANES), lambda h, i: (i, h)),
        out_shape=jax.ShapeDtypeStruct((m, heads * LANES), BF16),
        scratch_shapes=scratch,
        compiler_params=_params("arbitrary", "arbitrary"),
        name=name,
    )(*args)


def _na_kernel(q_ref, k_ref, v_ref, kc_ref, vc_ref, b_ref, o_ref, *, nb, rows):
    step = pl.program_id(1)
    kc = kc_ref[...]
    vc = vc_ref[...]
    tq, win = NA_QROWS * GRID_W, NA_WROWS * GRID_W
    last = rows // NA_QROWS - 1

    def one_block(bb, carry):
        b = step * nb + bb
        first_row = jnp.clip(b * NA_QROWS - NA_KH // 2, 0, rows - NA_WROWS)
        case = jnp.where(b == 0, 0, jnp.where(b == last, 2, 1))
        q_rows = pl.ds(pl.multiple_of(bb * tq, tq), tq)
        k_rows = pl.ds(pl.multiple_of(first_row * GRID_W, tq), win)
        q = q_ref[q_rows, :]
        s_win = _dot_nt(q, k_ref[k_rows, :]) + b_ref[0, case]
        s_ctx = _dot_nt(q, kc)
        m = jnp.maximum(jnp.max(s_win, axis=1, keepdims=True), jnp.max(s_ctx, axis=1, keepdims=True))
        p_win = jnp.exp2(s_win - m)
        p_ctx = jnp.exp2(s_ctx - m)
        l = jnp.sum(p_win, axis=1, keepdims=True) + jnp.sum(p_ctx, axis=1, keepdims=True)
        o = _dot(p_win.astype(BF16), v_ref[k_rows, :]) + _dot(p_ctx.astype(BF16), vc)
        o_ref[q_rows, :] = (o / l).astype(o_ref.dtype)
        return carry

    lax.fori_loop(0, nb, one_block, 0)


def _na_attention(q, k, v, v_col, kc, vc, vc_col, bias, *, heads, nb=4):
    s = q.shape[0]
    rows = s // GRID_W
    nb = min(nb, rows // NA_QROWS)
    assert s % GRID_W == 0 and rows >= NA_WROWS and rows % (NA_QROWS * nb) == 0
    nctx = kc.shape[0]
    vb, vcb = v_col // NA_DH, vc_col // NA_DH
    tq = nb * NA_QROWS * GRID_W
    return pl.pallas_call(
        functools.partial(_na_kernel, nb=nb, rows=rows),
        grid=(heads, s // tq),
        in_specs=[pl.BlockSpec((tq, NA_DH), lambda h, i: (i, h)),
                  pl.BlockSpec((s, NA_DH), lambda h, i: (0, h)),
                  pl.BlockSpec((s, NA_DH), lambda h, i: (0, vb + h)),
                  pl.BlockSpec((nctx, NA_DH), lambda h, i: (0, h)),
                  pl.BlockSpec((nctx, NA_DH), lambda h, i: (0, vcb + h)),
                  pl.BlockSpec((1,) + bias.shape[1:], lambda h, i: (h, 0, 0, 0))],
        out_specs=pl.BlockSpec((tq, NA_DH), lambda h, i: (i, h)),
        out_shape=jax.ShapeDtypeStruct((s, heads * NA_DH), BF16),
        compiler_params=_params("arbitrary", "arbitrary"),
        name="na_attention",
    )(q, k, v, kc, vc, bias)


def _merge_kernel(a_ref, b_ref, n_ref, hd_ref, wpa_ref, wpb_ref, wpc_ref, wga_ref, wgb_ref, wgc_ref,
                  bga_ref, bgb_ref, bgc_ref, o_ref):
    hd = hd_ref[...]

    def gated(o_ref_, wp_ref, wg_ref, bg_ref):
        z = _dot(hd, wg_ref[...]) + bg_ref[...]
        return (1.0 / (1.0 + jnp.exp(-z))) * _dot(o_ref_[...], wp_ref[...])

    out = gated(a_ref, wpa_ref, wga_ref, bga_ref) + gated(b_ref, wpb_ref, wgb_ref, bgb_ref)
    out = out + gated(n_ref, wpc_ref, wgc_ref, bgc_ref)
    o_ref[...] = out.astype(o_ref.dtype)


def _merge(a, b, n, hd, hd_col, wpa, wpb, wpc, wgu, bg, tm=1024, tn=512):
    m = a.shape[0]
    d = wpa.shape[1]
    rank = wgu.shape[0]
    tm, tn = min(tm, m), min(tn, d)
    assert m % tm == 0 and d % tn == 0 and hd_col % rank == 0
    nj = d // tn
    hb = hd_col // rank

    def act(w):
        return pl.BlockSpec((tm, w), lambda i, j: (i, 0))

    def wt(rows, off):
        return pl.BlockSpec((rows, tn), lambda i, j: (0, off * nj + j))

    return pl.pallas_call(
        _merge_kernel,
        grid=(m // tm, nj),
        in_specs=[act(a.shape[1]), act(b.shape[1]), act(n.shape[1]),
                  pl.BlockSpec((tm, rank), lambda i, j: (i, hb)),
                  wt(wpa.shape[0], 0), wt(wpb.shape[0], 0), wt(wpc.shape[0], 0),
                  wt(rank, 0), wt(rank, 1), wt(rank, 2), wt(1, 0), wt(1, 1), wt(1, 2)],
        out_specs=pl.BlockSpec((tm, tn), lambda i, j: (i, j)),
        out_shape=jax.ShapeDtypeStruct((m, d), BF16),
        compiler_params=_params("arbitrary", "arbitrary"),
        name="branch_merge",
    )(a, b, n, hd, wpa, wpb, wpc, wgu, wgu, wgu, bg, bg, bg)


def _rope_pattern(n_tokens):
    t = np.arange(n_tokens)
    row = jnp.asarray(t // GRID_W, F32)
    col = jnp.asarray(t % GRID_W, F32)
    axis_dim = MLA_ROPE // 2
    inv = ROPE_THETA ** (-jnp.arange(0, axis_dim, 2, dtype=F32) / axis_dim)
    ang_r, ang_c = row[:, None] * inv, col[:, None] * inv
    ang = jnp.concatenate([ang_r, ang_r, ang_c, ang_c], axis=1)
    return jnp.cos(ang), jnp.sin(ang)


def _rotate_half_matrix(width, lanes):
    p = np.zeros((width, width), np.float32)
    q = MLA_ROPE // 4
    for i in lanes:
        if (i % (2 * q)) < q:
            p[i + q, i] = -1.0
        else:
            p[i - q, i] = 1.0
    return p


def _group_matrix(width, group):
    idx = np.arange(width) // group
    return (idx[:, None] == idx[None, :]).astype(np.float32)


def _na_bias_table(rpb):
    heads = rpb.shape[0]
    edge = GRID_W - NA_KW
    rp = jnp.pad(rpb.astype(F32), ((0, 0), (NA_WROWS - 1, NA_WROWS - 1), (edge, edge)))
    by_col = jnp.stack([rp[:, :, GRID_W - 1 - q:2 * GRID_W - 1 - q] for q in range(GRID_W)], axis=2)
    qc = np.arange(GRID_W)[:, None]
    kcol = np.arange(GRID_W)[None, :]
    start = np.clip(qc - NA_KW // 2, 0, edge)
    valid_col = (kcol >= start) & (kcol < start + NA_KW)
    jr = np.arange(NA_WROWS)
    half = NA_KH // 2
    cases = [[(i, 0) for i in range(NA_QROWS)],
             [(half + i, i) for i in range(NA_QROWS)],
             [(NA_KH + i, NA_WROWS - NA_KH) for i in range(NA_QROWS)]]
    tables = []
    for case in cases:
        per_row = []
        for dq, dr0 in case:
            lo = NA_WROWS - 1 + NA_KH - 1 - dq
            valid = ((jr >= dr0) & (jr < dr0 + NA_KH))[:, None, None] & valid_col[None]
            tile = jnp.where(valid[None], by_col[:, lo:lo + NA_WROWS] * LOG2E, MASKED)
            per_row.append(tile.transpose(0, 2, 1, 3).reshape(heads, GRID_W, NA_WROWS * GRID_W))
        tables.append(jnp.concatenate(per_row, axis=1))
    return jnp.stack(tables, axis=1)


def kernel(x, c, ctx, c_ctx, w_ada_down, w_ada_up, b_ada, norm_attn_g, norm_mlp_g, w_in, diff_lambda, diff_q_norm, diff_k_norm, diff_subln, mla_q_a_norm, mla_kv_a_norm, w_mla_uq, w_mla_ukv, mla_q_norm, mla_k_norm, na_q_norm, na_k_norm, na_rpb, w_proj_a, w_proj_b, w_proj_c, w_gate_down, w_gate_up, b_gate, w_out, w_mlp_up, w_mlp_down):
    batch, seq, d = x.shape
    assert batch == 1
    depth = w_in.shape[0]
    hd_n = w_proj_a.shape[1] // DIFF_VD
    hm_n = w_proj_b.shape[1] // MLA_V
    hn_n = w_proj_c.shape[1] // NA_DH
    q_rank = mla_q_a_norm.shape[1]
    kv_rank = mla_kv_a_norm.shape[1]
    g_rank = w_gate_down.shape[2]
    wd, wm, wn = hd_n * DIFF_VD, hm_n * MLA_QK_PAD, hn_n * NA_DH
    kv_in = kv_rank + 2 * MLA_ROPE

    col_dq, col_dk, col_dv, col_cq = 0, wd, 2 * wd, 3 * wd
    col_nq = col_cq + q_rank
    col_nk, col_nv = col_nq + wn, col_nq + 2 * wn
    col_gd, col_kv = 0, g_rank + LANES

    cos64, sin64 = _rope_pattern(seq)
    ones64 = jnp.ones((seq, MLA_ROPE), F32)
    zeros64 = jnp.zeros((seq, MLA_ROPE), F32)
    cos_d, sin_d = jnp.tile(cos64, (1, 2)), jnp.tile(sin64, (1, 2))
    cos_m = jnp.concatenate([ones64, ones64, cos64, ones64], axis=1)
    sin_m = jnp.concatenate([zeros64, zeros64, sin64, zeros64], axis=1)
    pm_d = jnp.asarray(_rotate_half_matrix(LANES, range(LANES)), BF16)
    pm_m = jnp.asarray(_rotate_half_matrix(MLA_QK_PAD, range(MLA_NOPE, MLA_QK)), BF16)
    gm_d = jnp.asarray(_group_matrix(LANES, DIFF_DH), BF16)
    gm_m = jnp.asarray(_group_matrix(MLA_QK_PAD, MLA_QK_PAD), BF16)
    gm_n = jnp.asarray(_group_matrix(LANES, NA_DH), BF16)
    rot_d = np.abs(_rotate_half_matrix(LANES, range(LANES))).argmax(axis=0)

    cond = jnp.zeros((16, d), F32).at[0].set(c[0]).at[1].set(c_ctx)
    mod = _ada(cond, w_ada_down.astype(BF16), w_ada_up.astype(BF16), b_ada)

    def gain_row(g, reps, scale=1.0):
        return jnp.tile(g.astype(F32) * scale, reps)[None, :]

    def pad_heads(g):
        return jnp.concatenate([g.astype(F32), jnp.zeros((MLA_QK_PAD - MLA_QK,), F32)])

    xl, xc = x[0], ctx[0]
    for l in range(depth):
        need_ctx = l < depth - 1
        lam_init = 0.8 - 0.6 * math.exp(-0.3 * l)

        wi = w_in[l]
        sp = np.cumsum([0, wd, wd, wd, q_rank, kv_rank, MLA_ROPE, wn, wn, wn])
        part = [wi[:, sp[i]:sp[i + 1]] for i in range(9)]
        head_major = lambda w: w.reshape(d, 2, hd_n, DIFF_DH).transpose(0, 2, 1, 3).reshape(d, wd)
        w1a = jnp.concatenate([head_major(part[0]), head_major(part[1]), part[2], part[3],
                               part[6], part[7], part[8]], axis=1).astype(BF16)
        w1b = jnp.concatenate([w_gate_down[l], jnp.zeros((d, LANES), F32), part[4], part[5],
                               jnp.zeros((d, MLA_ROPE), F32)], axis=1).astype(BF16)
        uq = w_mla_uq[l].reshape(q_rank, hm_n, MLA_QK)
        w_uq = jnp.concatenate([uq, jnp.zeros((q_rank, hm_n, MLA_QK_PAD - MLA_QK), F32)],
                               axis=2).reshape(q_rank, wm).astype(BF16)
        ukv = w_mla_ukv[l].reshape(kv_rank, hm_n, MLA_NOPE + MLA_V)
        k_rows = jnp.concatenate([ukv[:, :, :MLA_NOPE], jnp.zeros((kv_rank, hm_n, MLA_QK_PAD - MLA_NOPE), F32)], axis=2)
        pe_rows = jnp.concatenate([jnp.zeros((MLA_ROPE, hm_n, MLA_NOPE), F32),
                                   jnp.broadcast_to(jnp.eye(MLA_ROPE, dtype=F32)[:, None, :], (MLA_ROPE, hm_n, MLA_ROPE)),
                                   jnp.zeros((MLA_ROPE, hm_n, MLA_QK_PAD - MLA_QK), F32)], axis=2)
        w_kv = jnp.concatenate([
            jnp.concatenate([k_rows.reshape(kv_rank, wm), ukv[:, :, MLA_NOPE:].reshape(kv_rank, hm_n * MLA_V)], axis=1),
            jnp.concatenate([pe_rows.reshape(MLA_ROPE, wm), jnp.zeros((MLA_ROPE, hm_n * MLA_V), F32)], axis=1),
            jnp.zeros((MLA_ROPE, wm + hm_n * MLA_V), F32)], axis=0).astype(BF16)
        kv_gain = jnp.concatenate([mla_kv_a_norm[l].astype(F32), jnp.ones((2 * MLA_ROPE,), F32)])[None, :]
        wpa, wpb, wpc = w_proj_a[l].astype(BF16), w_proj_b[l].astype(BF16), w_proj_c[l].astype(BF16)
        wgu, bg = w_gate_up[l].astype(BF16), b_gate[l][None, :]
        wo, wup, wdn = w_out[l].astype(BF16), w_mlp_up[l].astype(BF16), w_mlp_down[l].astype(BF16)

        dq_gain = gain_row(diff_q_norm[l], 2, DIFF_DH ** -0.5 * LOG2E)
        dk_gain = gain_row(diff_k_norm[l], 2)
        mq_gain = pad_heads(mla_q_norm[l])[None, :] * (MLA_QK ** -0.5 * LOG2E)
        mk_gain = pad_heads(mla_k_norm[l])[None, :]
        nq_gain = gain_row(na_q_norm[l], 1, NA_DH ** -0.5 * LOG2E)
        nk_gain = gain_row(na_k_norm[l], 1)
        rot_m = np.arange(MLA_QK_PAD)
        rot_m[MLA_NOPE:MLA_QK] = MLA_NOPE + rot_d[:MLA_ROPE]
        na_bias = _na_bias_table(na_rpb[l])
        lam_p = diff_lambda[l].astype(F32)
        sub_g = diff_subln[l].astype(F32)[None, :]

        def front(xs, mrow, rotate):
            m6 = [mod[l, mrow, i * d:(i + 1) * d][None, :] for i in range(N_MOD)]
            h = _modulate(xs, norm_attn_g[l][None, :], m6[0], m6[1])
            pa = _matmul(h, w1a, name="in_proj_a")
            pb = _matmul(h, w1b, tn=w1b.shape[1], name="in_proj_b")

            def rope_args(gain, rot_idx, pmat, cos, sin):
                return (gain[:, rot_idx], pmat, cos, sin) if rotate else None

            tile_d = lambda g: jnp.tile(g, (1, hd_n))
            dq = _headnorm(pa, col_dq, wd, wb=LANES, gain=tile_d(dq_gain), gmat=gm_d, inv_cnt=1.0 / DIFF_DH,
                           rope=rope_args(tile_d(dq_gain), np.tile(rot_d, hd_n) + np.repeat(np.arange(hd_n) * LANES, LANES), pm_d, cos_d, sin_d),
                           name="diff_q_prep")
            dk = _headnorm(pa, col_dk, wd, wb=LANES, gain=tile_d(dk_gain), gmat=gm_d, inv_cnt=1.0 / DIFF_DH,
                           rope=rope_args(tile_d(dk_gain), np.tile(rot_d, hd_n) + np.repeat(np.arange(hd_n) * LANES, LANES), pm_d, cos_d, sin_d),
                           name="diff_k_prep")
            q_raw = _matmul(pa, w_uq, a_col=col_cq, norm_gain=mla_q_a_norm[l].astype(F32)[None, :], n_norm=q_rank,
                            name="mla_q_up")
            kv_raw = _matmul(pb, w_kv, a_col=col_kv, norm_gain=kv_gain, n_norm=kv_rank, name="mla_kv_up")
            tile_m = lambda g: jnp.tile(g, (1, hm_n))
            rot_m_all = np.tile(rot_m, hm_n) + np.repeat(np.arange(hm_n) * MLA_QK_PAD, MLA_QK_PAD)
            mq = _headnorm(q_raw, 0, wm, wb=MLA_QK_PAD, gain=tile_m(mq_gain), gmat=gm_m, inv_cnt=1.0 / MLA_QK,
                           rope=rope_args(tile_m(mq_gain), rot_m_all, pm_m, cos_m, sin_m), tm=256, name="mla_q_prep")
            mk = _headnorm(kv_raw, 0, wm, wb=MLA_QK_PAD, gain=tile_m(mk_gain), gmat=gm_m, inv_cnt=1.0 / MLA_QK,
                           rope=rope_args(tile_m(mk_gain), rot_m_all, pm_m, cos_m, sin_m), tm=256, name="mla_k_prep")
            tile_n = lambda g: jnp.tile(g, (1, hn_n))
            nq = _headnorm(pa, col_nq, wn, wb=LANES, gain=tile_n(nq_gain), gmat=gm_n, inv_cnt=1.0 / NA_DH, name="na_q_prep")
            nk = _headnorm(pa, col_nk, wn, wb=LANES, gain=tile_n(nk_gain), gmat=gm_n, inv_cnt=1.0 / NA_DH, name="na_k_prep")
            return dict(m6=m6, h=h, pa=pa, pb=pb, dq=dq, dk=dk, mq=mq, mk=mk, kv_raw=kv_raw, nq=nq, nk=nk)

        def back(xs, f, o_a, o_b, o_c):
            m6 = f["m6"]
            merged = _merge(o_a, o_b, o_c, f["pb"], col_gd, wpa, wpb, wpc, wgu, bg)
            xs = _matmul(merged, wo, out_dtype=F32, epilogue="resid", resid=xs, gate=m6[2], tn=512, name="out_proj")
            h2 = _modulate(xs, norm_mlp_g[l][None, :], m6[3], m6[4])
            hid = _matmul(h2, wup, epilogue="relu2", name="mlp_up")
            return _matmul(hid, wdn, out_dtype=F32, epilogue="resid", resid=xs, gate=m6[5], tk=2048, name="mlp_down")

        fc = front(xc, 1, False)
        fl = front(xl, 0, True)
        dv_c, dv_l = fc["pa"][:, col_dv:col_dv + wd], fl["pa"][:, col_dv:col_dv + wd]
        mv_c, mv_l = fc["kv_raw"][:, wm:], fl["kv_raw"][:, wm:]
        dk_all = jnp.concatenate([fc["dk"], fl["dk"]], axis=0)
        dv_all = _with_ones(jnp.concatenate([dv_c, dv_l], axis=0), hd_n)
        mk_all = jnp.concatenate([fc["mk"], fl["mk"]], axis=0)
        mv_all = _with_ones(jnp.concatenate([mv_c, mv_l], axis=0), hm_n)
        diff_args = (lam_p, sub_g, lam_init)

        a_l = _flash(fl["dq"], dk_all, dv_all, heads=hd_n, dqk=DIFF_VD, diff=diff_args, tq=256, name="diff_attn")
        b_l = _flash(fl["mq"], mk_all, mv_all, heads=hm_n, dqk=MLA_QK_PAD, name="mla_attn")
        n_l = _na_attention(fl["nq"], fl["nk"], fl["pa"], col_nv, fc["nk"], fc["pa"], col_nv, na_bias, heads=hn_n)
        xl = back(xl, fl, a_l, b_l, n_l)
        if need_ctx:
            nv_c = fc["pa"][:, col_nv:col_nv + wn]
            a_c = _flash(fc["dq"], fc["dk"], _with_ones(dv_c, hd_n), heads=hd_n, dqk=DIFF_VD, diff=diff_args,
                         name="diff_attn_ctx")
            b_c = _flash(fc["mq"], fc["mk"], _with_ones(mv_c, hm_n), heads=hm_n, dqk=MLA_QK_PAD, name="mla_attn_ctx")
            n_c = _flash(fc["nq"], fc["nk"], _with_ones(nv_c, hn_n), heads=hn_n, dqk=NA_DH, name="na_attn_ctx")
            xc = back(xc, fc, a_c, b_c, n_c)
    return xl[None]
```

```python
import functools
import math

import numpy as np
import jax
import jax.numpy as jnp
from jax import lax
from jax.experimental import pallas as pl
from jax.experimental.pallas import tpu as pltpu

F32 = jnp.float32
BF16 = jnp.bfloat16

NORM_EPS = 1e-6
ROPE_THETA = 10000.0
GRID_W = 64
DIFF_DH = 64
DIFF_VD = 2 * DIFF_DH
MLA_NOPE = 128
MLA_ROPE = 64
MLA_QK = MLA_NOPE + MLA_ROPE
MLA_QK_PAD = 256
MLA_V = 128
NA_DH = 128
NA_KH = 8
NA_KW = 16
NA_QROWS = 4
NA_WROWS = 12
N_MOD = 6
N_BRANCH = 3
LOG2E = 1.4426950408889634
MASKED = -1e30
LANES = 128
ROW_CHUNK = 32
VMEM_LIMIT_BYTES = 56 * 1024 * 1024


def _params(*semantics):
    return pltpu.CompilerParams(dimension_semantics=semantics, vmem_limit_bytes=VMEM_LIMIT_BYTES)


def _dot(a, b):
    return jnp.dot(a, b, preferred_element_type=F32)


def _dot_nt(a, b):
    return lax.dot_general(a, b, (((1,), (1,)), ((), ())), preferred_element_type=F32)


def _ada_kernel(cond_ref, wd_ref, wu_ref, b_ref, o_ref):
    cnd = cond_ref[...]
    act = cnd / (1.0 + jnp.exp(-cnd))
    low = _dot(act.astype(BF16), wd_ref[...])
    o_ref[0] = _dot(low.astype(BF16), wu_ref[0]) + b_ref[0]


def _ada(cond, w_down, w_up, b):
    rows, d = cond.shape
    depth, rank, n = w_up.shape
    tn = min(n, 4096)
    return pl.pallas_call(
        _ada_kernel,
        grid=(depth, n // tn),
        in_specs=[pl.BlockSpec((rows, d), lambda l, j: (0, 0)),
                  pl.BlockSpec((d, rank), lambda l, j: (0, 0)),
                  pl.BlockSpec((1, rank, tn), lambda l, j: (l, 0, j)),
                  pl.BlockSpec((1, 1, tn), lambda l, j: (l, 0, j))],
        out_specs=pl.BlockSpec((1, rows, tn), lambda l, j: (l, 0, j)),
        out_shape=jax.ShapeDtypeStruct((depth, rows, n), F32),
        compiler_params=_params("arbitrary", "arbitrary"),
        name="ada_mod",
    )(cond, w_down, w_up, b.reshape(depth, 1, n))


def _modulate_kernel(x_ref, g_ref, sh_ref, sc_ref, o_ref):
    x = x_ref[...]
    r = lax.rsqrt(jnp.mean(x * x, axis=-1, keepdims=True) + NORM_EPS)
    y = (x * r) * g_ref[...]
    o_ref[...] = (y * (1.0 + sc_ref[...]) + sh_ref[...]).astype(o_ref.dtype)


def _modulate(x, g, shift, scale, tm=256):
    m, d = x.shape
    tm = min(tm, m)
    vec = pl.BlockSpec((1, d), lambda i: (0, 0))
    return pl.pallas_call(
        _modulate_kernel,
        grid=(m // tm,),
        in_specs=[pl.BlockSpec((tm, d), lambda i: (i, 0)), vec, vec, vec],
        out_specs=pl.BlockSpec((tm, d), lambda i: (i, 0)),
        out_shape=jax.ShapeDtypeStruct((m, d), BF16),
        compiler_params=_params("arbitrary"),
        name="modulate",
    )(x, g, shift, scale)


def _mm_kernel(*refs, nk, n_norm, epilogue):
    it = iter(refs)
    a_ref, w_ref = next(it), next(it)
    g_ref = next(it) if n_norm else None
    x_ref, gate_ref = (next(it), next(it)) if epilogue == "resid" else (None, None)
    o_ref = next(it)
    acc_ref = next(it) if nk > 1 else None

    a = a_ref[...]
    if n_norm:
        af = a.astype(F32)
        if n_norm < af.shape[1]:
            normed = lax.broadcasted_iota(jnp.int32, af.shape, 1) < n_norm
            ssq = jnp.sum(jnp.where(normed, af * af, 0.0), axis=-1, keepdims=True)
            r = lax.rsqrt(ssq * (1.0 / n_norm) + NORM_EPS)
            a = jnp.where(normed, (af * r) * g_ref[...], af).astype(BF16)
        else:
            r = lax.rsqrt(jnp.mean(af * af, axis=-1, keepdims=True) + NORM_EPS)
            a = ((af * r) * g_ref[...]).astype(BF16)
    part = _dot(a, w_ref[...])

    def finish(acc):
        if epilogue == "relu2":
            acc = jnp.maximum(acc, 0.0)
            acc = acc * acc
        elif epilogue == "resid":
            acc = x_ref[...] + gate_ref[...] * acc
        o_ref[...] = acc.astype(o_ref.dtype)

    if nk == 1:
        finish(part)
    else:
        k = pl.program_id(2)

        @pl.when(k == 0)
        def _():
            acc_ref[...] = part

        @pl.when(k > 0)
        def _():
            acc_ref[...] += part

        @pl.when(k == nk - 1)
        def _():
            finish(acc_ref[...])


def _matmul(a, w, *, a_col=0, out_dtype=BF16, norm_gain=None, n_norm=0, epilogue=None,
            resid=None, gate=None, tm=1024, tn=1024, tk=None, name="matmul"):
    m = a.shape[0]
    k_dim, n = w.shape
    tm, tn, tk = min(tm, m), min(tn, n), min(tk or k_dim, k_dim)
    assert m % tm == 0 and n % tn == 0 and k_dim % tk == 0 and a_col % tk == 0, (m, n, k_dim, tm, tn, tk, a_col)
    nk = k_dim // tk
    a_off = a_col // tk
    assert not (n_norm and nk > 1)
    in_specs = [pl.BlockSpec((tm, tk), lambda i, j, k: (i, a_off + k)),
                pl.BlockSpec((tk, tn), lambda i, j, k: (k, j))]
    args = [a, w]
    if n_norm:
        in_specs.append(pl.BlockSpec((1, tk), lambda i, j, k: (0, 0)))
        args.append(norm_gain)
    aliases = {}
    if epilogue == "resid":
        aliases = {len(args): 0}
        in_specs += [pl.BlockSpec((tm, tn), lambda i, j, k: (i, j)),
                     pl.BlockSpec((1, tn), lambda i, j, k: (0, j))]
        args += [resid, gate]
    return pl.pallas_call(
        functools.partial(_mm_kernel, nk=nk, n_norm=n_norm, epilogue=epilogue),
        grid=(m // tm, n // tn, nk),
        in_specs=in_specs,
        out_specs=pl.BlockSpec((tm, tn), lambda i, j, k: (i, j)),
        out_shape=jax.ShapeDtypeStruct((m, n), out_dtype),
        scratch_shapes=[pltpu.VMEM((tm, tn), F32)] if nk > 1 else [],
        input_output_aliases=aliases,
        compiler_params=_params("arbitrary", "arbitrary", "arbitrary"),
        name=name,
    )(*args)


def _headnorm_kernel(*refs, wb, nchunk, inv_cnt, rope):
    if rope:
        x_ref, g_ref, gm_ref, gr_ref, pm_ref, cos_ref, sin_ref, o_ref = refs
    else:
        x_ref, g_ref, gm_ref, o_ref = refs
    for c in range(nchunk):
        cols = slice(c * wb, (c + 1) * wb)
        xb = x_ref[:, cols]
        xf = xb.astype(F32)
        sq = xf * xf
        hi = sq.astype(BF16)
        lo = (sq - hi.astype(F32)).astype(BF16)
        ssq = _dot(hi, gm_ref[...]) + _dot(lo, gm_ref[...])
        r = lax.rsqrt(ssq * inv_cnt + NORM_EPS)
        y = (xf * r) * g_ref[:, cols]
        if rope:
            yr = (_dot(xb, pm_ref[...]) * r) * gr_ref[:, cols]
            y = y * cos_ref[...] + yr * sin_ref[...]
        o_ref[:, cols] = y.astype(o_ref.dtype)


def _headnorm(x, col, width, *, wb, gain, gmat, inv_cnt, rope=None, tm=512, name="headnorm"):
    m = x.shape[0]
    tm = min(tm, m)
    assert m % tm == 0 and col % width == 0 and width % wb == 0
    cb = col // width
    row = pl.BlockSpec((1, width), lambda i: (0, 0))
    mat = pl.BlockSpec((wb, wb), lambda i: (0, 0))
    in_specs = [pl.BlockSpec((tm, width), lambda i: (i, cb)), row, mat]
    args = [x, gain, gmat]
    if rope is not None:
        gain_rot, pmat, cos, sin = rope
        tab = pl.BlockSpec((tm, wb), lambda i: (i, 0))
        in_specs += [row, mat, tab, tab]
        args += [gain_rot, pmat, cos, sin]
    return pl.pallas_call(
        functools.partial(_headnorm_kernel, wb=wb, nchunk=width // wb, inv_cnt=inv_cnt, rope=rope is not None),
        grid=(m // tm,),
        in_specs=in_specs,
        out_specs=pl.BlockSpec((tm, width), lambda i: (i, 0)),
        out_shape=jax.ShapeDtypeStruct((m, width), BF16),
        compiler_params=_params("arbitrary"),
        name=name,
    )(*args)


def _flash_kernel(*refs, diff, keys_transposed, tq, tk, n, qb, lam_init):
    if diff:
        q_ref, k_ref, v_ref, lp_ref, sg_ref, o_ref, qs_ref, s0, s1, p0, p1, a0, a1, m_ref, acc_ref = refs
    else:
        q_ref, k_ref, v_ref, o_ref, s0, s1, p0, p1, a0, a1, m_ref, acc_ref = refs
    s_slots, p_slots, a_slots = (s0, s1), (p0, p1), (a0, a1)
    rows = m_ref.shape[0]

    def key_rows(j):
        off = j * tk
        return pl.ds(off if isinstance(j, int) else pl.multiple_of(off, 2 * LANES), tk)

    def softmax(slot):
        s_ref, p_ref, a_ref = s_slots[slot], p_slots[slot], a_slots[slot]
        for r in range(0, rows, ROW_CHUNK):
            rs = slice(r, r + ROW_CHUNK)
            m_prev = m_ref[rs, :]
            m_new = jnp.maximum(m_prev, jnp.max(s_ref[rs, :], axis=1, keepdims=True))
            a_ref[rs, :] = jnp.exp2(m_prev - m_new)
            m_ref[rs, :] = m_new
            p_ref[rs, :] = jnp.exp2(s_ref[rs, :] - m_new[:, :1]).astype(BF16)

    def values(j, slot):
        alpha = a_slots[slot][...]
        acc_ref[...] = (jnp.concatenate([alpha, alpha], axis=1) * acc_ref[...]
                        + _dot(p_slots[slot][...], v_ref[key_rows(j), :]))

    def one_block(b, carry):
        q_rows = pl.ds(b * tq if isinstance(b, int) else pl.multiple_of(b * tq, tq), tq)
        m_ref[...] = jnp.full(m_ref.shape, MASKED, F32)
        acc_ref[...] = jnp.zeros(acc_ref.shape, F32)
        if diff:
            q = q_ref[q_rows, :].astype(F32)
            first = lax.broadcasted_iota(jnp.int32, q.shape, 1) < DIFF_DH
            qs_ref[0:tq, :] = jnp.where(first, q, 0.0).astype(BF16)
            qs_ref[tq:2 * tq, :] = jnp.where(first, 0.0, q).astype(BF16)

        def scores(j, slot):
            q = qs_ref[...] if diff else q_ref[q_rows, :]
            if keys_transposed:
                s_slots[slot][...] = _dot(q, k_ref[0, j])
            else:
                s_slots[slot][...] = _dot_nt(q, k_ref[key_rows(j), :])

        def step(j, parity, do_scores, do_softmax, do_values):
            if do_scores:
                scores(j, parity)
            if do_softmax:
                softmax(1 - parity)
            if do_values:
                values(j - 2, parity)

        pairs = max(0, (n - 2) // 2)
        for j in range(min(2, n + 2)):
            step(j, j % 2, j < n, 1 <= j <= n, False)

        def pair(i, c):
            j = 2 + 2 * i
            step(j, 0, True, True, True)
            step(j + 1, 1, True, True, True)
            return c

        if pairs:
            lax.fori_loop(0, pairs, pair, 0)
        for j in range(2 + 2 * pairs, n + 2):
            step(j, j % 2, j < n, j <= n, True)

        acc = acc_ref[...]
        o = acc[:, :LANES] / acc[:, LANES:]
        if diff:
            lp = lp_ref[...]
            lam = (jnp.exp(jnp.sum(lp[0:1] * lp[1:2], axis=1, keepdims=True))
                   - jnp.exp(jnp.sum(lp[2:3] * lp[3:4], axis=1, keepdims=True)) + lam_init)
            d = o[0:tq] - lam * o[tq:2 * tq]
            r = lax.rsqrt(jnp.mean(d * d, axis=-1, keepdims=True) + NORM_EPS)
            o = ((d * r) * sg_ref[...]) * (1.0 - lam_init)
        o_ref[q_rows, :] = o.astype(o_ref.dtype)
        return carry

    if qb == 1:
        one_block(0, 0)
    else:
        lax.fori_loop(0, qb, one_block, 0)


def _with_ones(v, heads):
    v3 = v.reshape(v.shape[0], heads, LANES)
    return jnp.concatenate([v3, jnp.ones_like(v3)], axis=2).reshape(v.shape[0], 2 * heads * LANES)


def _key_chunks(k, heads, tk):
    skv, width = k.shape
    return k.reshape(skv // tk, tk, heads, width // heads).transpose(2, 0, 3, 1)


def _flash(q, k, v_ones, *, heads, dqk, diff=None, keys_transposed=False, tq=512, tk=1280, tokens=2048,
           name="flash"):
    m = q.shape[0]
    skv = k.shape[0]
    tq, tk = min(tq, m), min(tk, skv)
    tokens = max(tq, min(tokens, m))
    assert m % tokens == 0 and tokens % tq == 0 and skv % tk == 0 and k.shape[1] == heads * dqk
    assert tk % (2 * LANES) == 0
    n = skv // tk
    rows = 2 * tq if diff else tq
    assert rows % ROW_CHUNK == 0
    if keys_transposed:
        k_arg, k_spec = _key_chunks(k, heads, tk), pl.BlockSpec((1, n, dqk, tk), lambda h, i: (h, 0, 0, 0))
    else:
        k_arg, k_spec = k, pl.BlockSpec((skv, dqk), lambda h, i: (0, h))
    in_specs = [pl.BlockSpec((tokens, dqk), lambda h, i: (i, h)), k_spec,
                pl.BlockSpec((skv, 2 * LANES), lambda h, i: (0, h))]
    args = [q, k_arg, v_ones]
    scratch = []
    lam_init = 0.0
    if diff:
        lam_p, sub_g, lam_init = diff
        in_specs += [pl.BlockSpec(lam_p.shape, lambda h, i: (0, 0)),
                     pl.BlockSpec((1, LANES), lambda h, i: (0, 0))]
        args += [lam_p, sub_g]
        scratch.append(pltpu.VMEM((rows, dqk), BF16))
    scratch += [pltpu.VMEM((rows, tk), F32), pltpu.VMEM((rows, tk), F32),
                pltpu.VMEM((rows, tk), BF16), pltpu.VMEM((rows, tk), BF16),
                pltpu.VMEM((rows, LANES), F32), pltpu.VMEM((rows, LANES), F32),
                pltpu.VMEM((rows, LANES), F32), pltpu.VMEM((rows, 2 * LANES), F32)]
    return pl.pallas_call(
        functools.partial(_flash_kernel, diff=bool(diff), keys_transposed=keys_transposed, tq=tq, tk=tk, n=n,
                          qb=tokens // tq, lam_init=lam_init),
        grid=(heads, m // tokens),
        in_specs=in_specs,
        out_specs=pl.BlockSpec((tokens, LANES), lambda h, i: (i, h)),
        out_shape=jax.ShapeDtypeStruct((m, heads * LANES), BF16),
        scratch_shapes=scratch,
        compiler_params=_params("arbitrary", "arbitrary"),
        name=name,
    )(*args)


def _na_kernel(q_ref, k_ref, v_ref, kc_ref, vc_ref, b_ref, o_ref, *, nb, rows):
    step = pl.program_id(1)
    kc = kc_ref[...]
    vc = vc_ref[...]
    tq, win = NA_QROWS * GRID_W, NA_WROWS * GRID_W
    last = rows // NA_QROWS - 1

    def one_block(bb, carry):
        b = step * nb + bb
        first_row = jnp.clip(b * NA_QROWS - NA_KH // 2, 0, rows - NA_WROWS)
        case = jnp.where(b == 0, 0, jnp.where(b == last, 2, 1))
        q_rows = pl.ds(pl.multiple_of(bb * tq, tq), tq)
        k_rows = pl.ds(pl.multiple_of(first_row * GRID_W, tq), win)
        q = q_ref[q_rows, :]
        s_win = _dot_nt(q, k_ref[k_rows, :]) + b_ref[0, case]
        s_ctx = _dot_nt(q, kc)
        m = jnp.maximum(jnp.max(s_win, axis=1, keepdims=True), jnp.max(s_ctx, axis=1, keepdims=True))
        p_win = jnp.exp2(s_win - m)
        p_ctx = jnp.exp2(s_ctx - m)
        l = jnp.sum(p_win, axis=1, keepdims=True) + jnp.sum(p_ctx, axis=1, keepdims=True)
        o = _dot(p_win.astype(BF16), v_ref[k_rows, :]) + _dot(p_ctx.astype(BF16), vc)
        o_ref[q_rows, :] = (o / l).astype(o_ref.dtype)
        return carry

    lax.fori_loop(0, nb, one_block, 0)


def _na_attention(q, k, v, v_col, kc, vc, vc_col, bias, *, heads, nb=4):
    s = q.shape[0]
    rows = s // GRID_W
    nb = min(nb, rows // NA_QROWS)
    assert s % GRID_W == 0 and rows >= NA_WROWS and rows % (NA_QROWS * nb) == 0
    nctx = kc.shape[0]
    vb, vcb = v_col // NA_DH, vc_col // NA_DH
    tq = nb * NA_QROWS * GRID_W
    return pl.pallas_call(
        functools.partial(_na_kernel, nb=nb, rows=rows),
        grid=(heads, s // tq),
        in_specs=[pl.BlockSpec((tq, NA_DH), lambda h, i: (i, h)),
                  pl.BlockSpec((s, NA_DH), lambda h, i: (0, h)),
                  pl.BlockSpec((s, NA_DH), lambda h, i: (0, vb + h)),
                  pl.BlockSpec((nctx, NA_DH), lambda h, i: (0, h)),
                  pl.BlockSpec((nctx, NA_DH), lambda h, i: (0, vcb + h)),
                  pl.BlockSpec((1,) + bias.shape[1:], lambda h, i: (h, 0, 0, 0))],
        out_specs=pl.BlockSpec((tq, NA_DH), lambda h, i: (i, h)),
        out_shape=jax.ShapeDtypeStruct((s, heads * NA_DH), BF16),
        compiler_params=_params("arbitrary", "arbitrary"),
        name="na_attention",
    )(q, k, v, kc, vc, bias)


def _merge_kernel(a_ref, b_ref, n_ref, hd_ref, wpa_ref, wpb_ref, wpc_ref, wga_ref, wgb_ref, wgc_ref,
                  bga_ref, bgb_ref, bgc_ref, o_ref):
    hd = hd_ref[...]

    def gated(o_ref_, wp_ref, wg_ref, bg_ref):
        z = _dot(hd, wg_ref[...]) + bg_ref[...]
        return (1.0 / (1.0 + jnp.exp(-z))) * _dot(o_ref_[...], wp_ref[...])

    out = gated(a_ref, wpa_ref, wga_ref, bga_ref) + gated(b_ref, wpb_ref, wgb_ref, bgb_ref)
    out = out + gated(n_ref, wpc_ref, wgc_ref, bgc_ref)
    o_ref[...] = out.astype(o_ref.dtype)


def _merge(a, b, n, hd, hd_col, wpa, wpb, wpc, wgu, bg, tm=1024, tn=512):
    m = a.shape[0]
    d = wpa.shape[1]
    rank = wgu.shape[0]
    tm, tn = min(tm, m), min(tn, d)
    assert m % tm == 0 and d % tn == 0 and hd_col % rank == 0
    nj = d // tn
    hb = hd_col // rank

    def act(w):
        return pl.BlockSpec((tm, w), lambda i, j: (i, 0))

    def wt(rows, off):
        return pl.BlockSpec((rows, tn), lambda i, j: (0, off * nj + j))

    return pl.pallas_call(
        _merge_kernel,
        grid=(m // tm, nj),
        in_specs=[act(a.shape[1]), act(b.shape[1]), act(n.shape[1]),
                  pl.BlockSpec((tm, rank), lambda i, j: (i, hb)),
                  wt(wpa.shape[0], 0), wt(wpb.shape[0], 0), wt(wpc.shape[0], 0),
                  wt(rank, 0), wt(rank, 1), wt(rank, 2), wt(1, 0), wt(1, 1), wt(1, 2)],
        out_specs=pl.BlockSpec((tm, tn), lambda i, j: (i, j)),
        out_shape=jax.ShapeDtypeStruct((m, d), BF16),
        compiler_params=_params("arbitrary", "arbitrary"),
        name="branch_merge",
    )(a, b, n, hd, wpa, wpb, wpc, wgu, wgu, wgu, bg, bg, bg)


def _rope_pattern(n_tokens):
    t = np.arange(n_tokens)
    row = jnp.asarray(t // GRID_W, F32)
    col = jnp.asarray(t % GRID_W, F32)
    axis_dim = MLA_ROPE // 2
    inv = ROPE_THETA ** (-jnp.arange(0, axis_dim, 2, dtype=F32) / axis_dim)
    ang_r, ang_c = row[:, None] * inv, col[:, None] * inv
    ang = jnp.concatenate([ang_r, ang_r, ang_c, ang_c], axis=1)
    return jnp.cos(ang), jnp.sin(ang)


def _rotate_half_matrix(width, lanes):
    p = np.zeros((width, width), np.float32)
    q = MLA_ROPE // 4
    for i in lanes:
        if (i % (2 * q)) < q:
            p[i + q, i] = -1.0
        else:
            p[i - q, i] = 1.0
    return p


def _group_matrix(width, group):
    idx = np.arange(width) // group
    return (idx[:, None] == idx[None, :]).astype(np.float32)


def _na_bias_table(rpb):
    heads = rpb.shape[0]
    edge = GRID_W - NA_KW
    rp = jnp.pad(rpb.astype(F32), ((0, 0), (NA_WROWS - 1, NA_WROWS - 1), (edge, edge)))
    by_col = jnp.stack([rp[:, :, GRID_W - 1 - q:2 * GRID_W - 1 - q] for q in range(GRID_W)], axis=2)
    qc = np.arange(GRID_W)[:, None]
    kcol = np.arange(GRID_W)[None, :]
    start = np.clip(qc - NA_KW // 2, 0, edge)
    valid_col = (kcol >= start) & (kcol < start + NA_KW)
    jr = np.arange(NA_WROWS)
    half = NA_KH // 2
    cases = [[(i, 0) for i in range(NA_QROWS)],
             [(half + i, i) for i in range(NA_QROWS)],
             [(NA_KH + i, NA_WROWS - NA_KH) for i in range(NA_QROWS)]]
    tables = []
    for case in cases:
        per_row = []
        for dq, dr0 in case:
            lo = NA_WROWS - 1 + NA_KH - 1 - dq
            valid = ((jr >= dr0) & (jr < dr0 + NA_KH))[:, None, None] & valid_col[None]
            tile = jnp.where(valid[None], by_col[:, lo:lo + NA_WROWS] * LOG2E, MASKED)
            per_row.append(tile.transpose(0, 2, 1, 3).reshape(heads, GRID_W, NA_WROWS * GRID_W))
        tables.append(jnp.concatenate(per_row, axis=1))
    return jnp.stack(tables, axis=1)


def kernel(x, c, ctx, c_ctx, w_ada_down, w_ada_up, b_ada, norm_attn_g, norm_mlp_g, w_in, diff_lambda, diff_q_norm, diff_k_norm, diff_subln, mla_q_a_norm, mla_kv_a_norm, w_mla_uq, w_mla_ukv, mla_q_norm, mla_k_norm, na_q_norm, na_k_norm, na_rpb, w_proj_a, w_proj_b, w_proj_c, w_gate_down, w_gate_up, b_gate, w_out, w_mlp_up, w_mlp_down):
    batch, seq, d = x.shape
    assert batch == 1
    depth = w_in.shape[0]
    hd_n = w_proj_a.shape[1] // DIFF_VD
    hm_n = w_proj_b.shape[1] // MLA_V
    hn_n = w_proj_c.shape[1] // NA_DH
    q_rank = mla_q_a_norm.shape[1]
    kv_rank = mla_kv_a_norm.shape[1]
    g_rank = w_gate_down.shape[2]
    wd, wm, wn = hd_n * DIFF_VD, hm_n * MLA_QK_PAD, hn_n * NA_DH
    kv_in = kv_rank + 2 * MLA_ROPE

    col_dq, col_dk, col_dv, col_cq = 0, wd, 2 * wd, 3 * wd
    col_nq = col_cq + q_rank
    col_nk, col_nv = col_nq + wn, col_nq + 2 * wn
    col_gd, col_kv = 0, g_rank + LANES

    cos64, sin64 = _rope_pattern(seq)
    ones64 = jnp.ones((seq, MLA_ROPE), F32)
    zeros64 = jnp.zeros((seq, MLA_ROPE), F32)
    cos_d, sin_d = jnp.tile(cos64, (1, 2)), jnp.tile(sin64, (1, 2))
    cos_m = jnp.concatenate([ones64, ones64, cos64, ones64], axis=1)
    sin_m = jnp.concatenate([zeros64, zeros64, sin64, zeros64], axis=1)
    pm_d = jnp.asarray(_rotate_half_matrix(LANES, range(LANES)), BF16)
    pm_m = jnp.asarray(_rotate_half_matrix(MLA_QK_PAD, range(MLA_NOPE, MLA_QK)), BF16)
    gm_d = jnp.asarray(_group_matrix(LANES, DIFF_DH), BF16)
    gm_m = jnp.asarray(_group_matrix(MLA_QK_PAD, MLA_QK_PAD), BF16)
    gm_n = jnp.asarray(_group_matrix(LANES, NA_DH), BF16)
    rot_d = np.abs(_rotate_half_matrix(LANES, range(LANES))).argmax(axis=0)

    cond = jnp.zeros((16, d), F32).at[0].set(c[0]).at[1].set(c_ctx)
    mod = _ada(cond, w_ada_down.astype(BF16), w_ada_up.astype(BF16), b_ada)

    def gain_row(g, reps, scale=1.0):
        return jnp.tile(g.astype(F32) * scale, reps)[None, :]

    def pad_heads(g):
        return jnp.concatenate([g.astype(F32), jnp.zeros((MLA_QK_PAD - MLA_QK,), F32)])

    xl, xc = x[0], ctx[0]
    for l in range(depth):
        need_ctx = l < depth - 1
        lam_init = 0.8 - 0.6 * math.exp(-0.3 * l)

        wi = w_in[l]
        sp = np.cumsum([0, wd, wd, wd, q_rank, kv_rank, MLA_ROPE, wn, wn, wn])
        part = [wi[:, sp[i]:sp[i + 1]] for i in range(9)]
        head_major = lambda w: w.reshape(d, 2, hd_n, DIFF_DH).transpose(0, 2, 1, 3).reshape(d, wd)
        w1a = jnp.concatenate([head_major(part[0]), head_major(part[1]), part[2], part[3],
                               part[6], part[7], part[8]], axis=1).astype(BF16)
        w1b = jnp.concatenate([w_gate_down[l], jnp.zeros((d, LANES), F32), part[4], part[5],
                               jnp.zeros((d, MLA_ROPE), F32)], axis=1).astype(BF16)
        uq = w_mla_uq[l].reshape(q_rank, hm_n, MLA_QK)
        w_uq = jnp.concatenate([uq, jnp.zeros((q_rank, hm_n, MLA_QK_PAD - MLA_QK), F32)],
                               axis=2).reshape(q_rank, wm).astype(BF16)
        ukv = w_mla_ukv[l].reshape(kv_rank, hm_n, MLA_NOPE + MLA_V)
        k_rows = jnp.concatenate([ukv[:, :, :MLA_NOPE], jnp.zeros((kv_rank, hm_n, MLA_QK_PAD - MLA_NOPE), F32)], axis=2)
        pe_rows = jnp.concatenate([jnp.zeros((MLA_ROPE, hm_n, MLA_NOPE), F32),
                                   jnp.broadcast_to(jnp.eye(MLA_ROPE, dtype=F32)[:, None, :], (MLA_ROPE, hm_n, MLA_ROPE)),
                                   jnp.zeros((MLA_ROPE, hm_n, MLA_QK_PAD - MLA_QK), F32)], axis=2)
        w_kv = jnp.concatenate([
            jnp.concatenate([k_rows.reshape(kv_rank, wm), ukv[:, :, MLA_NOPE:].reshape(kv_rank, hm_n * MLA_V)], axis=1),
            jnp.concatenate([pe_rows.reshape(MLA_ROPE, wm), jnp.zeros((MLA_ROPE, hm_n * MLA_V), F32)], axis=1),
            jnp.zeros((MLA_ROPE, wm + hm_n * MLA_V), F32)], axis=0).astype(BF16)
        kv_gain = jnp.concatenate([mla_kv_a_norm[l].astype(F32), jnp.ones((2 * MLA_ROPE,), F32)])[None, :]
        wpa, wpb, wpc = w_proj_a[l].astype(BF16), w_proj_b[l].astype(BF16), w_proj_c[l].astype(BF16)
        wgu, bg = w_gate_up[l].astype(BF16), b_gate[l][None, :]
        wo, wup, wdn = w_out[l].astype(BF16), w_mlp_up[l].astype(BF16), w_mlp_down[l].astype(BF16)

        dq_gain = gain_row(diff_q_norm[l], 2, DIFF_DH ** -0.5 * LOG2E)
        dk_gain = gain_row(diff_k_norm[l], 2)
        mq_gain = pad_heads(mla_q_norm[l])[None, :] * (MLA_QK ** -0.5 * LOG2E)
        mk_gain = pad_heads(mla_k_norm[l])[None, :]
        nq_gain = gain_row(na_q_norm[l], 1, NA_DH ** -0.5 * LOG2E)
        nk_gain = gain_row(na_k_norm[l], 1)
        rot_m = np.arange(MLA_QK_PAD)
        rot_m[MLA_NOPE:MLA_QK] = MLA_NOPE + rot_d[:MLA_ROPE]
        na_bias = _na_bias_table(na_rpb[l])
        lam_p = diff_lambda[l].astype(F32)
        sub_g = diff_subln[l].astype(F32)[None, :]

        def front(xs, mrow, rotate):
            m6 = [mod[l, mrow, i * d:(i + 1) * d][None, :] for i in range(N_MOD)]
            h = _modulate(xs, norm_attn_g[l][None, :], m6[0], m6[1])
            pa = _matmul(h, w1a, name="in_proj_a")
            pb = _matmul(h, w1b, tn=w1b.shape[1], name="in_proj_b")

            def rope_args(gain, rot_idx, pmat, cos, sin):
                return (gain[:, rot_idx], pmat, cos, sin) if rotate else None

            tile_d = lambda g: jnp.tile(g, (1, hd_n))
            dq = _headnorm(pa, col_dq, wd, wb=LANES, gain=tile_d(dq_gain), gmat=gm_d, inv_cnt=1.0 / DIFF_DH,
                           rope=rope_args(tile_d(dq_gain), np.tile(rot_d, hd_n) + np.repeat(np.arange(hd_n) * LANES, LANES), pm_d, cos_d, sin_d),
                           name="diff_q_prep")
            dk = _headnorm(pa, col_dk, wd, wb=LANES, gain=tile_d(dk_gain), gmat=gm_d, inv_cnt=1.0 / DIFF_DH,
                           rope=rope_args(tile_d(dk_gain), np.tile(rot_d, hd_n) + np.repeat(np.arange(hd_n) * LANES, LANES), pm_d, cos_d, sin_d),
                           name="diff_k_prep")
            q_raw = _matmul(pa, w_uq, a_col=col_cq, norm_gain=mla_q_a_norm[l].astype(F32)[None, :], n_norm=q_rank,
                            name="mla_q_up")
            kv_raw = _matmul(pb, w_kv, a_col=col_kv, norm_gain=kv_gain, n_norm=kv_rank, name="mla_kv_up")
            tile_m = lambda g: jnp.tile(g, (1, hm_n))
            rot_m_all = np.tile(rot_m, hm_n) + np.repeat(np.arange(hm_n) * MLA_QK_PAD, MLA_QK_PAD)
            mq = _headnorm(q_raw, 0, wm, wb=MLA_QK_PAD, gain=tile_m(mq_gain), gmat=gm_m, inv_cnt=1.0 / MLA_QK,
                           rope=rope_args(tile_m(mq_gain), rot_m_all, pm_m, cos_m, sin_m), tm=256, name="mla_q_prep")
            mk = _headnorm(kv_raw, 0, wm, wb=MLA_QK_PAD, gain=tile_m(mk_gain), gmat=gm_m, inv_cnt=1.0 / MLA_QK,
                           rope=rope_args(tile_m(mk_gain), rot_m_all, pm_m, cos_m, sin_m), tm=256, name="mla_k_prep")
            tile_n = lambda g: jnp.tile(g, (1, hn_n))
            nq = _headnorm(pa, col_nq, wn, wb=LANES, gain=tile_n(nq_gain), gmat=gm_n, inv_cnt=1.0 / NA_DH, name="na_q_prep")
            nk = _headnorm(pa, col_nk, wn, wb=LANES, gain=tile_n(nk_gain), gmat=gm_n, inv_cnt=1.0 / NA_DH, name="na_k_prep")
            return dict(m6=m6, h=h, pa=pa, pb=pb, dq=dq, dk=dk, mq=mq, mk=mk, kv_raw=kv_raw, nq=nq, nk=nk)

        def back(xs, f, o_a, o_b, o_c):
            m6 = f["m6"]
            merged = _merge(o_a, o_b, o_c, f["pb"], col_gd, wpa, wpb, wpc, wgu, bg)
            xs = _matmul(merged, wo, out_dtype=F32, epilogue="resid", resid=xs, gate=m6[2], tn=512, name="out_proj")
            h2 = _modulate(xs, norm_mlp_g[l][None, :], m6[3], m6[4])
            hid = _matmul(h2, wup, epilogue="relu2", name="mlp_up")
            return _matmul(hid, wdn, out_dtype=F32, epilogue="resid", resid=xs, gate=m6[5], tk=2048, name="mlp_down")

        fc = front(xc, 1, False)
        fl = front(xl, 0, True)
        dv_c, dv_l = fc["pa"][:, col_dv:col_dv + wd], fl["pa"][:, col_dv:col_dv + wd]
        mv_c, mv_l = fc["kv_raw"][:, wm:], fl["kv_raw"][:, wm:]
        dk_all = jnp.concatenate([fc["dk"], fl["dk"]], axis=0)
        dv_all = _with_ones(jnp.concatenate([dv_c, dv_l], axis=0), hd_n)
        mk_all = jnp.concatenate([fc["mk"], fl["mk"]], axis=0)
        mv_all = _with_ones(jnp.concatenate([mv_c, mv_l], axis=0), hm_n)
        diff_args = (lam_p, sub_g, lam_init)

        a_l = _flash(fl["dq"], dk_all, dv_all, heads=hd_n, dqk=DIFF_VD, diff=diff_args, keys_transposed=True,
                     tq=256, name="diff_attn")
        b_l = _flash(fl["mq"], mk_all, mv_all, heads=hm_n, dqk=MLA_QK_PAD, name="mla_attn")
        n_l = _na_attention(fl["nq"], fl["nk"], fl["pa"], col_nv, fc["nk"], fc["pa"], col_nv, na_bias, heads=hn_n)
        xl = back(xl, fl, a_l, b_l, n_l)
        if need_ctx:
            nv_c = fc["pa"][:, col_nv:col_nv + wn]
            a_c = _flash(fc["dq"], fc["dk"], _with_ones(dv_c, hd_n), heads=hd_n, dqk=DIFF_VD, diff=diff_args,
                         name="diff_attn_ctx")
            b_c = _flash(fc["mq"], fc["mk"], _with_ones(mv_c, hm_n), heads=hm_n, dqk=MLA_QK_PAD, name="mla_attn_ctx")
            n_c = _flash(fc["nq"], fc["nk"], _with_ones(nv_c, hn_n), heads=hn_n, dqk=NA_DH, name="na_attn_ctx")
            xc = back(xc, fc, a_c, b_c, n_c)
    return xl[None]
```

```python
import functools
import math

import numpy as np
import jax
import jax.numpy as jnp
from jax import lax
from jax.experimental import pallas as pl
from jax.experimental.pallas import tpu as pltpu

F32 = jnp.float32
BF16 = jnp.bfloat16

NORM_EPS = 1e-6
ROPE_THETA = 10000.0
GRID_W = 64
DIFF_DH = 64
DIFF_VD = 2 * DIFF_DH
MLA_NOPE = 128
MLA_ROPE = 64
MLA_QK = MLA_NOPE + MLA_ROPE
MLA_QK_PAD = 256
MLA_V = 128
NA_DH = 128
NA_KH = 8
NA_KW = 16
NA_QROWS = 4
NA_WROWS = 12
N_MOD = 6
N_BRANCH = 3
LOG2E = 1.4426950408889634
MASKED = -1e30
LANES = 128
ROW_CHUNK = 32
VMEM_LIMIT_BYTES = 56 * 1024 * 1024


def _params(*semantics):
    return pltpu.CompilerParams(dimension_semantics=semantics, vmem_limit_bytes=VMEM_LIMIT_BYTES)


def _dot(a, b):
    return jnp.dot(a, b, preferred_element_type=F32)


def _dot_nt(a, b):
    return lax.dot_general(a, b, (((1,), (1,)), ((), ())), preferred_element_type=F32)


def _ada_kernel(cond_ref, wd_ref, wu_ref, b_ref, o_ref):
    cnd = cond_ref[...]
    act = cnd / (1.0 + jnp.exp(-cnd))
    low = _dot(act.astype(BF16), wd_ref[...])
    o_ref[0] = _dot(low.astype(BF16), wu_ref[0]) + b_ref[0]


def _ada(cond, w_down, w_up, b):
    rows, d = cond.shape
    depth, rank, n = w_up.shape
    tn = min(n, 4096)
    return pl.pallas_call(
        _ada_kernel,
        grid=(depth, n // tn),
        in_specs=[pl.BlockSpec((rows, d), lambda l, j: (0, 0)),
                  pl.BlockSpec((d, rank), lambda l, j: (0, 0)),
                  pl.BlockSpec((1, rank, tn), lambda l, j: (l, 0, j)),
                  pl.BlockSpec((1, 1, tn), lambda l, j: (l, 0, j))],
        out_specs=pl.BlockSpec((1, rows, tn), lambda l, j: (l, 0, j)),
        out_shape=jax.ShapeDtypeStruct((depth, rows, n), F32),
        compiler_params=_params("arbitrary", "arbitrary"),
        name="ada_mod",
    )(cond, w_down, w_up, b.reshape(depth, 1, n))


def _modulate_kernel(x_ref, g_ref, sh_ref, sc_ref, o_ref):
    x = x_ref[...]
    r = lax.rsqrt(jnp.mean(x * x, axis=-1, keepdims=True) + NORM_EPS)
    y = (x * r) * g_ref[...]
    o_ref[...] = (y * (1.0 + sc_ref[...]) + sh_ref[...]).astype(o_ref.dtype)


def _modulate(x, g, shift, scale, tm=256):
    m, d = x.shape
    tm = min(tm, m)
    vec = pl.BlockSpec((1, d), lambda i: (0, 0))
    return pl.pallas_call(
        _modulate_kernel,
        grid=(m // tm,),
        in_specs=[pl.BlockSpec((tm, d), lambda i: (i, 0)), vec, vec, vec],
        out_specs=pl.BlockSpec((tm, d), lambda i: (i, 0)),
        out_shape=jax.ShapeDtypeStruct((m, d), BF16),
        compiler_params=_params("arbitrary"),
        name="modulate",
    )(x, g, shift, scale)


def _mm_kernel(*refs, nk, n_norm, epilogue):
    it = iter(refs)
    a_ref, w_ref = next(it), next(it)
    g_ref = next(it) if n_norm else None
    x_ref, gate_ref = (next(it), next(it)) if epilogue == "resid" else (None, None)
    o_ref = next(it)
    acc_ref = next(it) if nk > 1 else None

    a = a_ref[...]
    if n_norm:
        af = a.astype(F32)
        if n_norm < af.shape[1]:
            normed = lax.broadcasted_iota(jnp.int32, af.shape, 1) < n_norm
            ssq = jnp.sum(jnp.where(normed, af * af, 0.0), axis=-1, keepdims=True)
            r = lax.rsqrt(ssq * (1.0 / n_norm) + NORM_EPS)
            a = jnp.where(normed, (af * r) * g_ref[...], af).astype(BF16)
        else:
            r = lax.rsqrt(jnp.mean(af * af, axis=-1, keepdims=True) + NORM_EPS)
            a = ((af * r) * g_ref[...]).astype(BF16)
    def finish(acc):
        if epilogue == "relu2":
            acc = jnp.maximum(acc, 0.0)
            acc = acc * acc
        elif epilogue == "resid":
            acc = x_ref[...] + gate_ref[...] * acc
        o_ref[...] = acc.astype(o_ref.dtype)

    if nk == 1:
        finish(_dot(a, w_ref[...]))
    else:
        k = pl.program_id(2)

        @pl.when(k == 0)
        def _():
            acc_ref[...] = jnp.zeros(acc_ref.shape, F32)

        acc_ref[...] += _dot(a, w_ref[...])

        @pl.when(k == nk - 1)
        def _():
            finish(acc_ref[...])


def _matmul(a, w, *, a_col=0, out_dtype=BF16, norm_gain=None, n_norm=0, epilogue=None,
            resid=None, gate=None, tm=1024, tn=1024, tk=None, name="matmul"):
    m = a.shape[0]
    k_dim, n = w.shape
    tm, tn, tk = min(tm, m), min(tn, n), min(tk or k_dim, k_dim)
    assert m % tm == 0 and n % tn == 0 and k_dim % tk == 0 and a_col % tk == 0, (m, n, k_dim, tm, tn, tk, a_col)
    nk = k_dim // tk
    a_off = a_col // tk
    assert not (n_norm and nk > 1)
    in_specs = [pl.BlockSpec((tm, tk), lambda i, j, k: (i, a_off + k)),
                pl.BlockSpec((tk, tn), lambda i, j, k: (k, j))]
    args = [a, w]
    if n_norm:
        in_specs.append(pl.BlockSpec((1, tk), lambda i, j, k: (0, 0)))
        args.append(norm_gain)
    aliases = {}
    if epilogue == "resid":
        aliases = {len(args): 0}
        in_specs += [pl.BlockSpec((tm, tn), lambda i, j, k: (i, j)),
                     pl.BlockSpec((1, tn), lambda i, j, k: (0, j))]
        args += [resid, gate]
    return pl.pallas_call(
        functools.partial(_mm_kernel, nk=nk, n_norm=n_norm, epilogue=epilogue),
        grid=(m // tm, n // tn, nk),
        in_specs=in_specs,
        out_specs=pl.BlockSpec((tm, tn), lambda i, j, k: (i, j)),
        out_shape=jax.ShapeDtypeStruct((m, n), out_dtype),
        scratch_shapes=[pltpu.VMEM((tm, tn), F32)] if nk > 1 else [],
        input_output_aliases=aliases,
        compiler_params=_params("arbitrary", "arbitrary", "arbitrary"),
        name=name,
    )(*args)


def _headnorm_kernel(*refs, wb, nchunk, inv_cnt, rope):
    if rope:
        x_ref, g_ref, gm_ref, gr_ref, pm_ref, cos_ref, sin_ref, o_ref = refs
    else:
        x_ref, g_ref, gm_ref, o_ref = refs
    for c in range(nchunk):
        cols = slice(c * wb, (c + 1) * wb)
        xb = x_ref[:, cols]
        xf = xb.astype(F32)
        sq = xf * xf
        hi = sq.astype(BF16)
        lo = (sq - hi.astype(F32)).astype(BF16)
        ssq = _dot(hi, gm_ref[...]) + _dot(lo, gm_ref[...])
        r = lax.rsqrt(ssq * inv_cnt + NORM_EPS)
        y = (xf * r) * g_ref[:, cols]
        if rope:
            yr = (_dot(xb, pm_ref[...]) * r) * gr_ref[:, cols]
            y = y * cos_ref[...] + yr * sin_ref[...]
        o_ref[:, cols] = y.astype(o_ref.dtype)


def _headnorm(x, col, width, *, wb, gain, gmat, inv_cnt, rope=None, tm=512, name="headnorm"):
    m = x.shape[0]
    tm = min(tm, m)
    assert m % tm == 0 and col % width == 0 and width % wb == 0
    cb = col // width
    row = pl.BlockSpec((1, width), lambda i: (0, 0))
    mat = pl.BlockSpec((wb, wb), lambda i: (0, 0))
    in_specs = [pl.BlockSpec((tm, width), lambda i: (i, cb)), row, mat]
    args = [x, gain, gmat]
    if rope is not None:
        gain_rot, pmat, cos, sin = rope
        tab = pl.BlockSpec((tm, wb), lambda i: (i, 0))
        in_specs += [row, mat, tab, tab]
        args += [gain_rot, pmat, cos, sin]
    return pl.pallas_call(
        functools.partial(_headnorm_kernel, wb=wb, nchunk=width // wb, inv_cnt=inv_cnt, rope=rope is not None),
        grid=(m // tm,),
        in_specs=in_specs,
        out_specs=pl.BlockSpec((tm, width), lambda i: (i, 0)),
        out_shape=jax.ShapeDtypeStruct((m, width), BF16),
        compiler_params=_params("arbitrary"),
        name=name,
    )(*args)


def _flash_kernel(*refs, diff, keys_transposed, tq, tk, n, qb, row_chunk, lam_init):
    if diff:
        q_ref, k_ref, v_ref, lp_ref, sg_ref, o_ref, qs_ref, s0, s1, p0, p1, a0, a1, m_ref, acc_ref = refs
    else:
        q_ref, k_ref, v_ref, o_ref, s0, s1, p0, p1, a0, a1, m_ref, acc_ref = refs
    s_slots, p_slots, a_slots = (s0, s1), (p0, p1), (a0, a1)
    rows = m_ref.shape[0]

    def key_rows(j):
        off = j * tk
        return pl.ds(off if isinstance(j, int) else pl.multiple_of(off, 2 * LANES), tk)

    def softmax(slot):
        s_ref, p_ref, a_ref = s_slots[slot], p_slots[slot], a_slots[slot]
        for r in range(0, rows, row_chunk):
            rs = slice(r, r + row_chunk)
            m_prev = m_ref[rs, :]
            m_new = jnp.maximum(m_prev, jnp.max(s_ref[rs, :], axis=1, keepdims=True))
            a_ref[rs, :] = jnp.exp2(m_prev - m_new)
            m_ref[rs, :] = m_new
            p_ref[rs, :] = jnp.exp2(s_ref[rs, :] - m_new[:, :1]).astype(BF16)

    def values(j, slot):
        alpha = a_slots[slot][...]
        acc_ref[...] = (jnp.concatenate([alpha, alpha], axis=1) * acc_ref[...]
                        + _dot(p_slots[slot][...], v_ref[key_rows(j), :]))

    def one_block(b, carry):
        q_rows = pl.ds(b * tq if isinstance(b, int) else pl.multiple_of(b * tq, tq), tq)
        m_ref[...] = jnp.full(m_ref.shape, MASKED, F32)
        acc_ref[...] = jnp.zeros(acc_ref.shape, F32)
        if diff:
            q = q_ref[q_rows, :].astype(F32)
            first = lax.broadcasted_iota(jnp.int32, q.shape, 1) < DIFF_DH
            qs_ref[0:tq, :] = jnp.where(first, q, 0.0).astype(BF16)
            qs_ref[tq:2 * tq, :] = jnp.where(first, 0.0, q).astype(BF16)

        def scores(j, slot):
            q = qs_ref[...] if diff else q_ref[q_rows, :]
            if keys_transposed:
                s_slots[slot][...] = _dot(q, k_ref[0, j])
            else:
                s_slots[slot][...] = _dot_nt(q, k_ref[key_rows(j), :])

        def step(j, parity, do_scores, do_softmax, do_values):
            if do_scores:
                scores(j, parity)
            if do_softmax:
                softmax(1 - parity)
            if do_values:
                values(j - 2, parity)

        pairs = max(0, (n - 2) // 2)
        for j in range(min(2, n + 2)):
            step(j, j % 2, j < n, 1 <= j <= n, False)

        def pair(i, c):
            j = 2 + 2 * i
            step(j, 0, True, True, True)
            step(j + 1, 1, True, True, True)
            return c

        if pairs:
            lax.fori_loop(0, pairs, pair, 0)
        for j in range(2 + 2 * pairs, n + 2):
            step(j, j % 2, j < n, j <= n, True)

        acc = acc_ref[...]
        o = acc[:, :LANES] / acc[:, LANES:]
        if diff:
            lp = lp_ref[...]
            lam = (jnp.exp(jnp.sum(lp[0:1] * lp[1:2], axis=1, keepdims=True))
                   - jnp.exp(jnp.sum(lp[2:3] * lp[3:4], axis=1, keepdims=True)) + lam_init)
            d = o[0:tq] - lam * o[tq:2 * tq]
            r = lax.rsqrt(jnp.mean(d * d, axis=-1, keepdims=True) + NORM_EPS)
            o = ((d * r) * sg_ref[...]) * (1.0 - lam_init)
        o_ref[q_rows, :] = o.astype(o_ref.dtype)
        return carry

    if qb == 1:
        one_block(0, 0)
    else:
        lax.fori_loop(0, qb, one_block, 0)


def _with_ones(v, heads):
    v3 = v.reshape(v.shape[0], heads, LANES)
    return jnp.concatenate([v3, jnp.ones_like(v3)], axis=2).reshape(v.shape[0], 2 * heads * LANES)


def _key_chunks(k, heads, tk):
    skv, width = k.shape
    return k.reshape(skv // tk, tk, heads, width // heads).transpose(2, 0, 3, 1)


def _flash(q, k, v_ones, *, heads, dqk, diff=None, keys_transposed=False, tq=512, tk=1280, tokens=2048,
           row_chunk=ROW_CHUNK, name="flash"):
    m = q.shape[0]
    skv = k.shape[0]
    tq, tk = min(tq, m), min(tk, skv)
    tokens = max(tq, min(tokens, m))
    assert m % tokens == 0 and tokens % tq == 0 and skv % tk == 0 and k.shape[1] == heads * dqk
    assert tk % (2 * LANES) == 0
    n = skv // tk
    rows = 2 * tq if diff else tq
    assert rows % row_chunk == 0
    if keys_transposed:
        k_arg, k_spec = _key_chunks(k, heads, tk), pl.BlockSpec((1, n, dqk, tk), lambda h, i: (h, 0, 0, 0))
    else:
        k_arg, k_spec = k, pl.BlockSpec((skv, dqk), lambda h, i: (0, h))
    in_specs = [pl.BlockSpec((tokens, dqk), lambda h, i: (i, h)), k_spec,
                pl.BlockSpec((skv, 2 * LANES), lambda h, i: (0, h))]
    args = [q, k_arg, v_ones]
    scratch = []
    lam_init = 0.0
    if diff:
        lam_p, sub_g, lam_init = diff
        in_specs += [pl.BlockSpec(lam_p.shape, lambda h, i: (0, 0)),
                     pl.BlockSpec((1, LANES), lambda h, i: (0, 0))]
        args += [lam_p, sub_g]
        scratch.append(pltpu.VMEM((rows, dqk), BF16))
    scratch += [pltpu.VMEM((rows, tk), F32), pltpu.VMEM((rows, tk), F32),
                pltpu.VMEM((rows, tk), BF16), pltpu.VMEM((rows, tk), BF16),
                pltpu.VMEM((rows, LANES), F32), pltpu.VMEM((rows, LANES), F32),
                pltpu.VMEM((rows, LANES), F32), pltpu.VMEM((rows, 2 * LANES), F32)]
    return pl.pallas_call(
        functools.partial(_flash_kernel, diff=bool(diff), keys_transposed=keys_transposed, tq=tq, tk=tk, n=n,
                          qb=tokens // tq, row_chunk=row_chunk, lam_init=lam_init),
        grid=(heads, m // tokens),
        in_specs=in_specs,
        out_specs=pl.BlockSpec((tokens, LANES), lambda h, i: (i, h)),
        out_shape=jax.ShapeDtypeStruct((m, heads * LANES), BF16),
        scratch_shapes=scratch,
        compiler_params=_params("arbitrary", "arbitrary"),
        name=name,
    )(*args)


def _na_kernel(q_ref, k_ref, v_ref, kc_ref, vc_ref, b_ref, o_ref, *, nb, rows):
    step = pl.program_id(1)
    kc = kc_ref[...]
    vc = vc_ref[...]
    tq, win = NA_QROWS * GRID_W, NA_WROWS * GRID_W
    last = rows // NA_QROWS - 1

    def one_block(bb, carry):
        b = step * nb + bb
        first_row = jnp.clip(b * NA_QROWS - NA_KH // 2, 0, rows - NA_WROWS)
        case = jnp.where(b == 0, 0, jnp.where(b == last, 2, 1))
        q_rows = pl.ds(pl.multiple_of(bb * tq, tq), tq)
        k_rows = pl.ds(pl.multiple_of(first_row * GRID_W, tq), win)
        q = q_ref[q_rows, :]
        s_win = _dot_nt(q, k_ref[k_rows, :]) + b_ref[0, case]
        s_ctx = _dot_nt(q, kc)
        m = jnp.maximum(jnp.max(s_win, axis=1, keepdims=True), jnp.max(s_ctx, axis=1, keepdims=True))
        p_win = jnp.exp2(s_win - m)
        p_ctx = jnp.exp2(s_ctx - m)
        l = jnp.sum(p_win, axis=1, keepdims=True) + jnp.sum(p_ctx, axis=1, keepdims=True)
        o = _dot(p_win.astype(BF16), v_ref[k_rows, :]) + _dot(p_ctx.astype(BF16), vc)
        o_ref[q_rows, :] = (o / l).astype(o_ref.dtype)
        return carry

    lax.fori_loop(0, nb, one_block, 0)


def _na_attention(q, k, v, v_col, kc, vc, vc_col, bias, *, heads, nb=4):
    s = q.shape[0]
    rows = s // GRID_W
    nb = min(nb, rows // NA_QROWS)
    assert s % GRID_W == 0 and rows >= NA_WROWS and rows % (NA_QROWS * nb) == 0
    nctx = kc.shape[0]
    vb, vcb = v_col // NA_DH, vc_col // NA_DH
    tq = nb * NA_QROWS * GRID_W
    return pl.pallas_call(
        functools.partial(_na_kernel, nb=nb, rows=rows),
        grid=(heads, s // tq),
        in_specs=[pl.BlockSpec((tq, NA_DH), lambda h, i: (i, h)),
                  pl.BlockSpec((s, NA_DH), lambda h, i: (0, h)),
                  pl.BlockSpec((s, NA_DH), lambda h, i: (0, vb + h)),
                  pl.BlockSpec((nctx, NA_DH), lambda h, i: (0, h)),
                  pl.BlockSpec((nctx, NA_DH), lambda h, i: (0, vcb + h)),
                  pl.BlockSpec((1,) + bias.shape[1:], lambda h, i: (h, 0, 0, 0))],
        out_specs=pl.BlockSpec((tq, NA_DH), lambda h, i: (i, h)),
        out_shape=jax.ShapeDtypeStruct((s, heads * NA_DH), BF16),
        compiler_params=_params("arbitrary", "arbitrary"),
        name="na_attention",
    )(q, k, v, kc, vc, bias)


def _merge_kernel(a_ref, b_ref, n_ref, hd_ref, wpa_ref, wpb_ref, wpc_ref, wga_ref, wgb_ref, wgc_ref,
                  bga_ref, bgb_ref, bgc_ref, o_ref):
    hd = hd_ref[...]

    def gated(o_ref_, wp_ref, wg_ref, bg_ref):
        z = _dot(hd, wg_ref[...]) + bg_ref[...]
        return (1.0 / (1.0 + jnp.exp(-z))) * _dot(o_ref_[...], wp_ref[...])

    out = gated(a_ref, wpa_ref, wga_ref, bga_ref) + gated(b_ref, wpb_ref, wgb_ref, bgb_ref)
    out = out + gated(n_ref, wpc_ref, wgc_ref, bgc_ref)
    o_ref[...] = out.astype(o_ref.dtype)


def _merge(a, b, n, hd, hd_col, wpa, wpb, wpc, wgu, bg, tm=1024, tn=512):
    m = a.shape[0]
    d = wpa.shape[1]
    rank = wgu.shape[0]
    tm, tn = min(tm, m), min(tn, d)
    assert m % tm == 0 and d % tn == 0 and hd_col % rank == 0
    nj = d // tn
    hb = hd_col // rank

    def act(w):
        return pl.BlockSpec((tm, w), lambda i, j: (i, 0))

    def wt(rows, off):
        return pl.BlockSpec((rows, tn), lambda i, j: (0, off * nj + j))

    return pl.pallas_call(
        _merge_kernel,
        grid=(m // tm, nj),
        in_specs=[act(a.shape[1]), act(b.shape[1]), act(n.shape[1]),
                  pl.BlockSpec((tm, rank), lambda i, j: (i, hb)),
                  wt(wpa.shape[0], 0), wt(wpb.shape[0], 0), wt(wpc.shape[0], 0),
                  wt(rank, 0), wt(rank, 1), wt(rank, 2), wt(1, 0), wt(1, 1), wt(1, 2)],
        out_specs=pl.BlockSpec((tm, tn), lambda i, j: (i, j)),
        out_shape=jax.ShapeDtypeStruct((m, d), BF16),
        compiler_params=_params("arbitrary", "arbitrary"),
        name="branch_merge",
    )(a, b, n, hd, wpa, wpb, wpc, wgu, wgu, wgu, bg, bg, bg)


def _rope_pattern(n_tokens):
    t = np.arange(n_tokens)
    row = jnp.asarray(t // GRID_W, F32)
    col = jnp.asarray(t % GRID_W, F32)
    axis_dim = MLA_ROPE // 2
    inv = ROPE_THETA ** (-jnp.arange(0, axis_dim, 2, dtype=F32) / axis_dim)
    ang_r, ang_c = row[:, None] * inv, col[:, None] * inv
    ang = jnp.concatenate([ang_r, ang_r, ang_c, ang_c], axis=1)
    return jnp.cos(ang), jnp.sin(ang)


def _rotate_half_matrix(width, lanes):
    p = np.zeros((width, width), np.float32)
    q = MLA_ROPE // 4
    for i in lanes:
        if (i % (2 * q)) < q:
            p[i + q, i] = -1.0
        else:
            p[i - q, i] = 1.0
    return p


def _group_matrix(width, group):
    idx = np.arange(width) // group
    return (idx[:, None] == idx[None, :]).astype(np.float32)


def _na_bias_table(rpb):
    heads = rpb.shape[0]
    edge = GRID_W - NA_KW
    rp = jnp.pad(rpb.astype(F32), ((0, 0), (NA_WROWS - 1, NA_WROWS - 1), (edge, edge)))
    by_col = jnp.stack([rp[:, :, GRID_W - 1 - q:2 * GRID_W - 1 - q] for q in range(GRID_W)], axis=2)
    qc = np.arange(GRID_W)[:, None]
    kcol = np.arange(GRID_W)[None, :]
    start = np.clip(qc - NA_KW // 2, 0, edge)
    valid_col = (kcol >= start) & (kcol < start + NA_KW)
    jr = np.arange(NA_WROWS)
    half = NA_KH // 2
    cases = [[(i, 0) for i in range(NA_QROWS)],
             [(half + i, i) for i in range(NA_QROWS)],
             [(NA_KH + i, NA_WROWS - NA_KH) for i in range(NA_QROWS)]]
    tables = []
    for case in cases:
        per_row = []
        for dq, dr0 in case:
            lo = NA_WROWS - 1 + NA_KH - 1 - dq
            valid = ((jr >= dr0) & (jr < dr0 + NA_KH))[:, None, None] & valid_col[None]
            tile = jnp.where(valid[None], by_col[:, lo:lo + NA_WROWS] * LOG2E, MASKED)
            per_row.append(tile.transpose(0, 2, 1, 3).reshape(heads, GRID_W, NA_WROWS * GRID_W))
        tables.append(jnp.concatenate(per_row, axis=1))
    return jnp.stack(tables, axis=1)


def kernel(x, c, ctx, c_ctx, w_ada_down, w_ada_up, b_ada, norm_attn_g, norm_mlp_g, w_in, diff_lambda, diff_q_norm, diff_k_norm, diff_subln, mla_q_a_norm, mla_kv_a_norm, w_mla_uq, w_mla_ukv, mla_q_norm, mla_k_norm, na_q_norm, na_k_norm, na_rpb, w_proj_a, w_proj_b, w_proj_c, w_gate_down, w_gate_up, b_gate, w_out, w_mlp_up, w_mlp_down):
    batch, seq, d = x.shape
    assert batch == 1
    depth = w_in.shape[0]
    hd_n = w_proj_a.shape[1] // DIFF_VD
    hm_n = w_proj_b.shape[1] // MLA_V
    hn_n = w_proj_c.shape[1] // NA_DH
    q_rank = mla_q_a_norm.shape[1]
    kv_rank = mla_kv_a_norm.shape[1]
    g_rank = w_gate_down.shape[2]
    wd, wm, wn = hd_n * DIFF_VD, hm_n * MLA_QK_PAD, hn_n * NA_DH
    kv_in = kv_rank + 2 * MLA_ROPE

    col_dq, col_dk, col_dv, col_cq = 0, wd, 2 * wd, 3 * wd
    col_nq = col_cq + q_rank
    col_nk, col_nv = col_nq + wn, col_nq + 2 * wn
    col_gd, col_kv = 0, g_rank + LANES

    cos64, sin64 = _rope_pattern(seq)
    ones64 = jnp.ones((seq, MLA_ROPE), F32)
    zeros64 = jnp.zeros((seq, MLA_ROPE), F32)
    cos_d, sin_d = jnp.tile(cos64, (1, 2)), jnp.tile(sin64, (1, 2))
    cos_m = jnp.concatenate([ones64, ones64, cos64, ones64], axis=1)
    sin_m = jnp.concatenate([zeros64, zeros64, sin64, zeros64], axis=1)
    pm_d = jnp.asarray(_rotate_half_matrix(LANES, range(LANES)), BF16)
    pm_m = jnp.asarray(_rotate_half_matrix(MLA_QK_PAD, range(MLA_NOPE, MLA_QK)), BF16)
    gm_d = jnp.asarray(_group_matrix(LANES, DIFF_DH), BF16)
    gm_m = jnp.asarray(_group_matrix(MLA_QK_PAD, MLA_QK_PAD), BF16)
    gm_n = jnp.asarray(_group_matrix(LANES, NA_DH), BF16)
    rot_d = np.abs(_rotate_half_matrix(LANES, range(LANES))).argmax(axis=0)

    cond = jnp.zeros((16, d), F32).at[0].set(c[0]).at[1].set(c_ctx)
    mod = _ada(cond, w_ada_down.astype(BF16), w_ada_up.astype(BF16), b_ada)

    def gain_row(g, reps, scale=1.0):
        return jnp.tile(g.astype(F32) * scale, reps)[None, :]

    def pad_heads(g):
        return jnp.concatenate([g.astype(F32), jnp.zeros((MLA_QK_PAD - MLA_QK,), F32)])

    xl, xc = x[0], ctx[0]
    for l in range(depth):
        need_ctx = l < depth - 1
        lam_init = 0.8 - 0.6 * math.exp(-0.3 * l)

        wi = w_in[l]
        sp = np.cumsum([0, wd, wd, wd, q_rank, kv_rank, MLA_ROPE, wn, wn, wn])
        part = [wi[:, sp[i]:sp[i + 1]] for i in range(9)]
        head_major = lambda w: w.reshape(d, 2, hd_n, DIFF_DH).transpose(0, 2, 1, 3).reshape(d, wd)
        w1a = jnp.concatenate([head_major(part[0]), head_major(part[1]), part[2], part[3],
                               part[6], part[7], part[8]], axis=1).astype(BF16)
        w1b = jnp.concatenate([w_gate_down[l], jnp.zeros((d, LANES), F32), part[4], part[5],
                               jnp.zeros((d, MLA_ROPE), F32)], axis=1).astype(BF16)
        uq = w_mla_uq[l].reshape(q_rank, hm_n, MLA_QK)
        w_uq = jnp.concatenate([uq, jnp.zeros((q_rank, hm_n, MLA_QK_PAD - MLA_QK), F32)],
                               axis=2).reshape(q_rank, wm).astype(BF16)
        ukv = w_mla_ukv[l].reshape(kv_rank, hm_n, MLA_NOPE + MLA_V)
        k_rows = jnp.concatenate([ukv[:, :, :MLA_NOPE], jnp.zeros((kv_rank, hm_n, MLA_QK_PAD - MLA_NOPE), F32)], axis=2)
        pe_rows = jnp.concatenate([jnp.zeros((MLA_ROPE, hm_n, MLA_NOPE), F32),
                                   jnp.broadcast_to(jnp.eye(MLA_ROPE, dtype=F32)[:, None, :], (MLA_ROPE, hm_n, MLA_ROPE)),
                                   jnp.zeros((MLA_ROPE, hm_n, MLA_QK_PAD - MLA_QK), F32)], axis=2)
        w_kv = jnp.concatenate([
            jnp.concatenate([k_rows.reshape(kv_rank, wm), ukv[:, :, MLA_NOPE:].reshape(kv_rank, hm_n * MLA_V)], axis=1),
            jnp.concatenate([pe_rows.reshape(MLA_ROPE, wm), jnp.zeros((MLA_ROPE, hm_n * MLA_V), F32)], axis=1),
            jnp.zeros((MLA_ROPE, wm + hm_n * MLA_V), F32)], axis=0).astype(BF16)
        kv_gain = jnp.concatenate([mla_kv_a_norm[l].astype(F32), jnp.ones((2 * MLA_ROPE,), F32)])[None, :]
        wpa, wpb, wpc = w_proj_a[l].astype(BF16), w_proj_b[l].astype(BF16), w_proj_c[l].astype(BF16)
        wgu, bg = w_gate_up[l].astype(BF16), b_gate[l][None, :]
        wo, wup, wdn = w_out[l].astype(BF16), w_mlp_up[l].astype(BF16), w_mlp_down[l].astype(BF16)

        dq_gain = gain_row(diff_q_norm[l], 2, DIFF_DH ** -0.5 * LOG2E)
        dk_gain = gain_row(diff_k_norm[l], 2)
        mq_gain = pad_heads(mla_q_norm[l])[None, :] * (MLA_QK ** -0.5 * LOG2E)
        mk_gain = pad_heads(mla_k_norm[l])[None, :]
        nq_gain = gain_row(na_q_norm[l], 1, NA_DH ** -0.5 * LOG2E)
        nk_gain = gain_row(na_k_norm[l], 1)
        rot_m = np.arange(MLA_QK_PAD)
        rot_m[MLA_NOPE:MLA_QK] = MLA_NOPE + rot_d[:MLA_ROPE]
        na_bias = _na_bias_table(na_rpb[l])
        lam_p = diff_lambda[l].astype(F32)
        sub_g = diff_subln[l].astype(F32)[None, :]

        def front(xs, mrow, rotate):
            m6 = [mod[l, mrow, i * d:(i + 1) * d][None, :] for i in range(N_MOD)]
            h = _modulate(xs, norm_attn_g[l][None, :], m6[0], m6[1])
            pa = _matmul(h, w1a, name="in_proj_a")
            pb = _matmul(h, w1b, tn=w1b.shape[1], name="in_proj_b")

            def rope_args(gain, rot_idx, pmat, cos, sin):
                return (gain[:, rot_idx], pmat, cos, sin) if rotate else None

            tile_d = lambda g: jnp.tile(g, (1, hd_n))
            dq = _headnorm(pa, col_dq, wd, wb=LANES, gain=tile_d(dq_gain), gmat=gm_d, inv_cnt=1.0 / DIFF_DH,
                           rope=rope_args(tile_d(dq_gain), np.tile(rot_d, hd_n) + np.repeat(np.arange(hd_n) * LANES, LANES), pm_d, cos_d, sin_d),
                           name="diff_q_prep")
            dk = _headnorm(pa, col_dk, wd, wb=LANES, gain=tile_d(dk_gain), gmat=gm_d, inv_cnt=1.0 / DIFF_DH,
                           rope=rope_args(tile_d(dk_gain), np.tile(rot_d, hd_n) + np.repeat(np.arange(hd_n) * LANES, LANES), pm_d, cos_d, sin_d),
                           name="diff_k_prep")
            q_raw = _matmul(pa, w_uq, a_col=col_cq, norm_gain=mla_q_a_norm[l].astype(F32)[None, :], n_norm=q_rank,
                            name="mla_q_up")
            kv_raw = _matmul(pb, w_kv, a_col=col_kv, norm_gain=kv_gain, n_norm=kv_rank, name="mla_kv_up")
            tile_m = lambda g: jnp.tile(g, (1, hm_n))
            rot_m_all = np.tile(rot_m, hm_n) + np.repeat(np.arange(hm_n) * MLA_QK_PAD, MLA_QK_PAD)
            mq = _headnorm(q_raw, 0, wm, wb=MLA_QK_PAD, gain=tile_m(mq_gain), gmat=gm_m, inv_cnt=1.0 / MLA_QK,
                           rope=rope_args(tile_m(mq_gain), rot_m_all, pm_m, cos_m, sin_m), tm=256, name="mla_q_prep")
            mk = _headnorm(kv_raw, 0, wm, wb=MLA_QK_PAD, gain=tile_m(mk_gain), gmat=gm_m, inv_cnt=1.0 / MLA_QK,
                           rope=rope_args(tile_m(mk_gain), rot_m_all, pm_m, cos_m, sin_m), tm=256, name="mla_k_prep")
            tile_n = lambda g: jnp.tile(g, (1, hn_n))
            nq = _headnorm(pa, col_nq, wn, wb=LANES, gain=tile_n(nq_gain), gmat=gm_n, inv_cnt=1.0 / NA_DH, name="na_q_prep")
            nk = _headnorm(pa, col_nk, wn, wb=LANES, gain=tile_n(nk_gain), gmat=gm_n, inv_cnt=1.0 / NA_DH, name="na_k_prep")
            return dict(m6=m6, h=h, pa=pa, pb=pb, dq=dq, dk=dk, mq=mq, mk=mk, kv_raw=kv_raw, nq=nq, nk=nk)

        def back(xs, f, o_a, o_b, o_c):
            m6 = f["m6"]
            merged = _merge(o_a, o_b, o_c, f["pb"], col_gd, wpa, wpb, wpc, wgu, bg)
            xs = _matmul(merged, wo, out_dtype=F32, epilogue="resid", resid=xs, gate=m6[2], tn=512, name="out_proj")
            h2 = _modulate(xs, norm_mlp_g[l][None, :], m6[3], m6[4])
            hid = _matmul(h2, wup, epilogue="relu2", name="mlp_up")
            return _matmul(hid, wdn, out_dtype=F32, epilogue="resid", resid=xs, gate=m6[5], tk=2048, name="mlp_down")

        fc = front(xc, 1, False)
        fl = front(xl, 0, True)
        dv_c, dv_l = fc["pa"][:, col_dv:col_dv + wd], fl["pa"][:, col_dv:col_dv + wd]
        mv_c, mv_l = fc["kv_raw"][:, wm:], fl["kv_raw"][:, wm:]
        dk_all = jnp.concatenate([fc["dk"], fl["dk"]], axis=0)
        dv_all = _with_ones(jnp.concatenate([dv_c, dv_l], axis=0), hd_n)
        mk_all = jnp.concatenate([fc["mk"], fl["mk"]], axis=0)
        mv_all = _with_ones(jnp.concatenate([mv_c, mv_l], axis=0), hm_n)
        diff_args = (lam_p, sub_g, lam_init)

        a_l = _flash(fl["dq"], dk_all, dv_all, heads=hd_n, dqk=DIFF_VD, diff=diff_args, keys_transposed=True,
                     tq=256, row_chunk=2 * ROW_CHUNK, name="diff_attn")
        b_l = _flash(fl["mq"], mk_all, mv_all, heads=hm_n, dqk=MLA_QK_PAD, name="mla_attn")
        n_l = _na_attention(fl["nq"], fl["nk"], fl["pa"], col_nv, fc["nk"], fc["pa"], col_nv, na_bias, heads=hn_n)
        xl = back(xl, fl, a_l, b_l, n_l)
        if need_ctx:
            nv_c = fc["pa"][:, col_nv:col_nv + wn]
            a_c = _flash(fc["dq"], fc["dk"], _with_ones(dv_c, hd_n), heads=hd_n, dqk=DIFF_VD, diff=diff_args,
                         name="diff_attn_ctx")
            b_c = _flash(fc["mq"], fc["mk"], _with_ones(mv_c, hm_n), heads=hm_n, dqk=MLA_QK_PAD, name="mla_attn_ctx")
            n_c = _flash(fc["nq"], fc["nk"], _with_ones(nv_c, hn_n), heads=hn_n, dqk=NA_DH, name="na_attn_ctx")
            xc = back(xc, fc, a_c, b_c, n_c)
    return xl[None]
```

```python
import functools
import math

import numpy as np
import jax
import jax.numpy as jnp
from jax import lax
from jax.experimental import pallas as pl
from jax.experimental.pallas import tpu as pltpu

F32 = jnp.float32
BF16 = jnp.bfloat16

NORM_EPS = 1e-6
ROPE_THETA = 10000.0
GRID_W = 64
DIFF_DH = 64
DIFF_VD = 2 * DIFF_DH
MLA_NOPE = 128
MLA_ROPE = 64
MLA_QK = MLA_NOPE + MLA_ROPE
MLA_QK_PAD = 256
MLA_V = 128
NA_DH = 128
NA_KH = 8
NA_KW = 16
NA_QROWS = 4
NA_WROWS = 12
N_MOD = 6
LOG2E = 1.4426950408889634
MASKED = -1e30
LANES = 128
ROW_CHUNK = 32
VMEM_LIMIT_BYTES = 56 * 1024 * 1024


def _params(*semantics):
    return pltpu.CompilerParams(dimension_semantics=semantics, vmem_limit_bytes=VMEM_LIMIT_BYTES)


def _dot(a, b):
    return jnp.dot(a, b, preferred_element_type=F32)


def _dot_nt(a, b):
    return lax.dot_general(a, b, (((1,), (1,)), ((), ())), preferred_element_type=F32)


def _ada_kernel(cond_ref, wd_ref, wu_ref, b_ref, o_ref):
    cnd = cond_ref[...]
    act = cnd / (1.0 + jnp.exp(-cnd))
    low = _dot(act.astype(BF16), wd_ref[...])
    o_ref[0] = _dot(low.astype(BF16), wu_ref[0]) + b_ref[0]


def _ada(cond, w_down, w_up, b):
    rows, d = cond.shape
    depth, rank, n = w_up.shape
    tn = min(n, 4096)
    return pl.pallas_call(
        _ada_kernel,
        grid=(depth, n // tn),
        in_specs=[pl.BlockSpec((rows, d), lambda l, j: (0, 0)),
                  pl.BlockSpec((d, rank), lambda l, j: (0, 0)),
                  pl.BlockSpec((1, rank, tn), lambda l, j: (l, 0, j)),
                  pl.BlockSpec((1, 1, tn), lambda l, j: (l, 0, j))],
        out_specs=pl.BlockSpec((1, rows, tn), lambda l, j: (l, 0, j)),
        out_shape=jax.ShapeDtypeStruct((depth, rows, n), F32),
        compiler_params=_params("arbitrary", "arbitrary"),
        name="ada_mod",
    )(cond, w_down, w_up, b.reshape(depth, 1, n))


def _modulate_kernel(x_ref, g_ref, sh_ref, sc_ref, o_ref):
    x = x_ref[...]
    r = lax.rsqrt(jnp.mean(x * x, axis=-1, keepdims=True) + NORM_EPS)
    y = (x * r) * g_ref[...]
    o_ref[...] = (y * (1.0 + sc_ref[...]) + sh_ref[...]).astype(o_ref.dtype)


def _modulate(x, g, shift, scale, tm=512):
    m, d = x.shape
    tm = min(tm, m)
    vec = pl.BlockSpec((1, d), lambda i: (0, 0))
    return pl.pallas_call(
        _modulate_kernel,
        grid=(m // tm,),
        in_specs=[pl.BlockSpec((tm, d), lambda i: (i, 0)), vec, vec, vec],
        out_specs=pl.BlockSpec((tm, d), lambda i: (i, 0)),
        out_shape=jax.ShapeDtypeStruct((m, d), BF16),
        compiler_params=_params("arbitrary"),
        name="modulate",
    )(x, g, shift, scale)


def _mm_kernel(*refs, nk, n_norm, epilogue):
    it = iter(refs)
    a_ref, w_ref = next(it), next(it)
    g_ref = next(it) if n_norm else None
    x_ref, gate_ref = (next(it), next(it)) if epilogue == "resid" else (None, None)
    o_ref = next(it)
    acc_ref = next(it) if nk > 1 else None

    a = a_ref[...]
    if n_norm:
        af = a.astype(F32)
        if n_norm < af.shape[1]:
            normed = lax.broadcasted_iota(jnp.int32, af.shape, 1) < n_norm
            ssq = jnp.sum(jnp.where(normed, af * af, 0.0), axis=-1, keepdims=True)
            r = lax.rsqrt(ssq * (1.0 / n_norm) + NORM_EPS)
            a = jnp.where(normed, (af * r) * g_ref[...], af).astype(BF16)
        else:
            r = lax.rsqrt(jnp.mean(af * af, axis=-1, keepdims=True) + NORM_EPS)
            a = ((af * r) * g_ref[...]).astype(BF16)
    def finish(acc):
        if epilogue == "relu2":
            acc = jnp.maximum(acc, 0.0)
            acc = acc * acc
        elif epilogue == "resid":
            acc = x_ref[...] + gate_ref[...] * acc
        o_ref[...] = acc.astype(o_ref.dtype)

    if nk == 1:
        finish(_dot(a, w_ref[...]))
    else:
        k = pl.program_id(2)

        @pl.when(k == 0)
        def _():
            acc_ref[...] = jnp.zeros(acc_ref.shape, F32)

        acc_ref[...] += _dot(a, w_ref[...])

        @pl.when(k == nk - 1)
        def _():
            finish(acc_ref[...])


def _matmul(a, w, *, a_col=0, out_dtype=BF16, norm_gain=None, n_norm=0, epilogue=None,
            resid=None, gate=None, tm=1024, tn=1024, tk=None, name="matmul"):
    m = a.shape[0]
    k_dim, n = w.shape
    tm, tn, tk = min(tm, m), min(tn, n), min(tk or k_dim, k_dim)
    assert m % tm == 0 and n % tn == 0 and k_dim % tk == 0 and a_col % tk == 0, (m, n, k_dim, tm, tn, tk, a_col)
    nk = k_dim // tk
    a_off = a_col // tk
    assert not (n_norm and nk > 1)
    in_specs = [pl.BlockSpec((tm, tk), lambda i, j, k: (i, a_off + k)),
                pl.BlockSpec((tk, tn), lambda i, j, k: (k, j))]
    args = [a, w]
    if n_norm:
        in_specs.append(pl.BlockSpec((1, tk), lambda i, j, k: (0, 0)))
        args.append(norm_gain)
    aliases = {}
    if epilogue == "resid":
        aliases = {len(args): 0}
        in_specs += [pl.BlockSpec((tm, tn), lambda i, j, k: (i, j)),
                     pl.BlockSpec((1, tn), lambda i, j, k: (0, j))]
        args += [resid, gate]
    return pl.pallas_call(
        functools.partial(_mm_kernel, nk=nk, n_norm=n_norm, epilogue=epilogue),
        grid=(m // tm, n // tn, nk),
        in_specs=in_specs,
        out_specs=pl.BlockSpec((tm, tn), lambda i, j, k: (i, j)),
        out_shape=jax.ShapeDtypeStruct((m, n), out_dtype),
        scratch_shapes=[pltpu.VMEM((tm, tn), F32)] if nk > 1 else [],
        input_output_aliases=aliases,
        compiler_params=_params("arbitrary", "arbitrary", "arbitrary"),
        name=name,
    )(*args)


def _headnorm_kernel(*refs, wb, nchunk, inv_cnt, rope):
    if rope:
        x_ref, g_ref, gm_ref, gr_ref, pm_ref, cos_ref, sin_ref, o_ref = refs
    else:
        x_ref, g_ref, gm_ref, o_ref = refs
    for c in range(nchunk):
        cols = slice(c * wb, (c + 1) * wb)
        xb = x_ref[:, cols]
        xf = xb.astype(F32)
        sq = xf * xf
        hi = sq.astype(BF16)
        lo = (sq - hi.astype(F32)).astype(BF16)
        ssq = _dot(hi, gm_ref[...]) + _dot(lo, gm_ref[...])
        r = lax.rsqrt(ssq * inv_cnt + NORM_EPS)
        y = (xf * r) * g_ref[:, cols]
        if rope:
            yr = (_dot(xb, pm_ref[...]) * r) * gr_ref[:, cols]
            y = y * cos_ref[...] + yr * sin_ref[...]
        o_ref[:, cols] = y.astype(o_ref.dtype)


def _headnorm(x, col, width, *, wb, gain, gmat, inv_cnt, rope=None, tm=1024, name="headnorm"):
    m = x.shape[0]
    tm = min(tm, m)
    assert m % tm == 0 and col % width == 0 and width % wb == 0
    cb = col // width
    row = pl.BlockSpec((1, width), lambda i: (0, 0))
    mat = pl.BlockSpec((wb, wb), lambda i: (0, 0))
    in_specs = [pl.BlockSpec((tm, width), lambda i: (i, cb)), row, mat]
    args = [x, gain, gmat]
    if rope is not None:
        gain_rot, pmat, cos, sin = rope
        tab = pl.BlockSpec((tm, wb), lambda i: (i, 0))
        in_specs += [row, mat, tab, tab]
        args += [gain_rot, pmat, cos, sin]
    return pl.pallas_call(
        functools.partial(_headnorm_kernel, wb=wb, nchunk=width // wb, inv_cnt=inv_cnt, rope=rope is not None),
        grid=(m // tm,),
        in_specs=in_specs,
        out_specs=pl.BlockSpec((tm, width), lambda i: (i, 0)),
        out_shape=jax.ShapeDtypeStruct((m, width), BF16),
        compiler_params=_params("arbitrary"),
        name=name,
    )(*args)


def _flash_kernel(*refs, diff, keys_transposed, tq, tk, n, qb, row_chunk, lam_init):
    if diff:
        q_ref, k_ref, v_ref, lp_ref, sg_ref, o_ref, qs_ref, s0, s1, p0, p1, a0, a1, m_ref, acc_ref = refs
    else:
        q_ref, k_ref, v_ref, o_ref, s0, s1, p0, p1, a0, a1, m_ref, acc_ref = refs
    s_slots, p_slots, a_slots = (s0, s1), (p0, p1), (a0, a1)
    rows = m_ref.shape[0]

    def key_rows(j):
        off = j * tk
        return pl.ds(off if isinstance(j, int) else pl.multiple_of(off, 2 * LANES), tk)

    def softmax(slot):
        s_ref, p_ref, a_ref = s_slots[slot], p_slots[slot], a_slots[slot]
        for r in range(0, rows, row_chunk):
            rs = slice(r, r + row_chunk)
            m_prev = m_ref[rs, :]
            m_new = jnp.maximum(m_prev, jnp.max(s_ref[rs, :], axis=1, keepdims=True))
            a_ref[rs, :] = jnp.exp2(m_prev - m_new)
            m_ref[rs, :] = m_new
            p_ref[rs, :] = jnp.exp2(s_ref[rs, :] - m_new[:, :1]).astype(BF16)

    def values(j, slot):
        alpha = a_slots[slot][...]
        acc_ref[...] = (jnp.concatenate([alpha, alpha], axis=1) * acc_ref[...]
                        + _dot(p_slots[slot][...], v_ref[key_rows(j), :]))

    def one_block(b, carry):
        q_rows = pl.ds(b * tq if isinstance(b, int) else pl.multiple_of(b * tq, tq), tq)
        m_ref[...] = jnp.full(m_ref.shape, MASKED, F32)
        acc_ref[...] = jnp.zeros(acc_ref.shape, F32)
        if diff:
            q = q_ref[q_rows, :].astype(F32)
            first = lax.broadcasted_iota(jnp.int32, q.shape, 1) < DIFF_DH
            qs_ref[0:tq, :] = jnp.where(first, q, 0.0).astype(BF16)
            qs_ref[tq:2 * tq, :] = jnp.where(first, 0.0, q).astype(BF16)

        def scores(j, slot):
            q = qs_ref[...] if diff else q_ref[q_rows, :]
            if keys_transposed:
                s_slots[slot][...] = _dot(q, k_ref[0, j])
            else:
                s_slots[slot][...] = _dot_nt(q, k_ref[key_rows(j), :])

        def step(j, parity, do_scores, do_softmax, do_values):
            if do_scores:
                scores(j, parity)
            if do_softmax:
                softmax(1 - parity)
            if do_values:
                values(j - 2, parity)

        pairs = max(0, (n - 2) // 2)
        for j in range(min(2, n + 2)):
            step(j, j % 2, j < n, 1 <= j <= n, False)

        def pair(i, c):
            j = 2 + 2 * i
            step(j, 0, True, True, True)
            step(j + 1, 1, True, True, True)
            return c

        if pairs:
            lax.fori_loop(0, pairs, pair, 0)
        for j in range(2 + 2 * pairs, n + 2):
            step(j, j % 2, j < n, j <= n, True)

        acc = acc_ref[...]
        o = acc[:, :LANES] / acc[:, LANES:]
        if diff:
            lp = lp_ref[...]
            lam = (jnp.exp(jnp.sum(lp[0:1] * lp[1:2], axis=1, keepdims=True))
                   - jnp.exp(jnp.sum(lp[2:3] * lp[3:4], axis=1, keepdims=True)) + lam_init)
            d = o[0:tq] - lam * o[tq:2 * tq]
            r = lax.rsqrt(jnp.mean(d * d, axis=-1, keepdims=True) + NORM_EPS)
            o = ((d * r) * sg_ref[...]) * (1.0 - lam_init)
        o_ref[q_rows, :] = o.astype(o_ref.dtype)
        return carry

    if qb == 1:
        one_block(0, 0)
    else:
        lax.fori_loop(0, qb, one_block, 0)


def _with_ones(v, heads):
    v3 = v.reshape(v.shape[0], heads, LANES)
    return jnp.concatenate([v3, jnp.ones_like(v3)], axis=2).reshape(v.shape[0], 2 * heads * LANES)


def _key_chunks(k, heads, tk):
    skv, width = k.shape
    return k.reshape(skv // tk, tk, heads, width // heads).transpose(2, 0, 3, 1)


def _flash(q, k, v_ones, *, heads, dqk, diff=None, keys_transposed=False, tq=512, tk=1280, tokens=2048,
           row_chunk=ROW_CHUNK, name="flash"):
    m = q.shape[0]
    skv = k.shape[0]
    tq, tk = min(tq, m), min(tk, skv)
    tokens = max(tq, min(tokens, m))
    assert m % tokens == 0 and tokens % tq == 0 and skv % tk == 0 and k.shape[1] == heads * dqk
    assert tk % (2 * LANES) == 0
    n = skv // tk
    rows = 2 * tq if diff else tq
    assert rows % row_chunk == 0
    if keys_transposed:
        k_arg, k_spec = _key_chunks(k, heads, tk), pl.BlockSpec((1, n, dqk, tk), lambda h, i: (h, 0, 0, 0))
    else:
        k_arg, k_spec = k, pl.BlockSpec((skv, dqk), lambda h, i: (0, h))
    in_specs = [pl.BlockSpec((tokens, dqk), lambda h, i: (i, h)), k_spec,
                pl.BlockSpec((skv, 2 * LANES), lambda h, i: (0, h))]
    args = [q, k_arg, v_ones]
    scratch = []
    lam_init = 0.0
    if diff:
        lam_p, sub_g, lam_init = diff
        in_specs += [pl.BlockSpec(lam_p.shape, lambda h, i: (0, 0)),
                     pl.BlockSpec((1, LANES), lambda h, i: (0, 0))]
        args += [lam_p, sub_g]
        scratch.append(pltpu.VMEM((rows, dqk), BF16))
    scratch += [pltpu.VMEM((rows, tk), F32), pltpu.VMEM((rows, tk), F32),
                pltpu.VMEM((rows, tk), BF16), pltpu.VMEM((rows, tk), BF16),
                pltpu.VMEM((rows, LANES), F32), pltpu.VMEM((rows, LANES), F32),
                pltpu.VMEM((rows, LANES), F32), pltpu.VMEM((rows, 2 * LANES), F32)]
    return pl.pallas_call(
        functools.partial(_flash_kernel, diff=bool(diff), keys_transposed=keys_transposed, tq=tq, tk=tk, n=n,
                          qb=tokens // tq, row_chunk=row_chunk, lam_init=lam_init),
        grid=(heads, m // tokens),
        in_specs=in_specs,
        out_specs=pl.BlockSpec((tokens, LANES), lambda h, i: (i, h)),
        out_shape=jax.ShapeDtypeStruct((m, heads * LANES), BF16),
        scratch_shapes=scratch,
        compiler_params=_params("arbitrary", "arbitrary"),
        name=name,
    )(*args)


def _na_kernel(q_ref, k_ref, v_ref, kc_ref, vc_ref, b_ref, o_ref, *, nb, rows):
    step = pl.program_id(1)
    kc = kc_ref[...]
    vc = vc_ref[...]
    tq, win = NA_QROWS * GRID_W, NA_WROWS * GRID_W
    last = rows // NA_QROWS - 1

    def one_block(bb, carry):
        b = step * nb + bb
        first_row = jnp.clip(b * NA_QROWS - NA_KH // 2, 0, rows - NA_WROWS)
        case = jnp.where(b == 0, 0, jnp.where(b == last, 2, 1))
        q_rows = pl.ds(pl.multiple_of(bb * tq, tq), tq)
        k_rows = pl.ds(pl.multiple_of(first_row * GRID_W, tq), win)
        q = q_ref[q_rows, :]
        s_win = _dot_nt(q, k_ref[k_rows, :]) + b_ref[0, case]
        s_ctx = _dot_nt(q, kc)
        m = jnp.maximum(jnp.max(s_win, axis=1, keepdims=True), jnp.max(s_ctx, axis=1, keepdims=True))
        p_win = jnp.exp2(s_win - m)
        p_ctx = jnp.exp2(s_ctx - m)
        l = jnp.sum(p_win, axis=1, keepdims=True) + jnp.sum(p_ctx, axis=1, keepdims=True)
        o = _dot(p_win.astype(BF16), v_ref[k_rows, :]) + _dot(p_ctx.astype(BF16), vc)
        o_ref[q_rows, :] = (o / l).astype(o_ref.dtype)
        return carry

    lax.fori_loop(0, nb, one_block, 0)


def _na_attention(q, k, v, v_col, kc, vc, vc_col, bias, *, heads, nb=4):
    s = q.shape[0]
    rows = s // GRID_W
    nb = min(nb, rows // NA_QROWS)
    assert s % GRID_W == 0 and rows >= NA_WROWS and rows % (NA_QROWS * nb) == 0
    nctx = kc.shape[0]
    vb, vcb = v_col // NA_DH, vc_col // NA_DH
    tq = nb * NA_QROWS * GRID_W
    return pl.pallas_call(
        functools.partial(_na_kernel, nb=nb, rows=rows),
        grid=(heads, s // tq),
        in_specs=[pl.BlockSpec((tq, NA_DH), lambda h, i: (i, h)),
                  pl.BlockSpec((s, NA_DH), lambda h, i: (0, h)),
                  pl.BlockSpec((s, NA_DH), lambda h, i: (0, vb + h)),
                  pl.BlockSpec((nctx, NA_DH), lambda h, i: (0, h)),
                  pl.BlockSpec((nctx, NA_DH), lambda h, i: (0, vcb + h)),
                  pl.BlockSpec((1,) + bias.shape[1:], lambda h, i: (h, 0, 0, 0))],
        out_specs=pl.BlockSpec((tq, NA_DH), lambda h, i: (i, h)),
        out_shape=jax.ShapeDtypeStruct((s, heads * NA_DH), BF16),
        compiler_params=_params("arbitrary", "arbitrary"),
        name="na_attention",
    )(q, k, v, kc, vc, bias)


def _merge_kernel(a_ref, b_ref, n_ref, hd_ref, wpa_ref, wpb_ref, wpc_ref, wga_ref, wgb_ref, wgc_ref,
                  bga_ref, bgb_ref, bgc_ref, o_ref):
    hd = hd_ref[...]

    def gated(o_ref_, wp_ref, wg_ref, bg_ref):
        z = _dot(hd, wg_ref[...]) + bg_ref[...]
        return (1.0 / (1.0 + jnp.exp(-z))) * _dot(o_ref_[...], wp_ref[...])

    out = gated(a_ref, wpa_ref, wga_ref, bga_ref) + gated(b_ref, wpb_ref, wgb_ref, bgb_ref)
    out = out + gated(n_ref, wpc_ref, wgc_ref, bgc_ref)
    o_ref[...] = out.astype(o_ref.dtype)


def _merge(a, b, n, hd, hd_col, wpa, wpb, wpc, wgu, bg, tm=1024, tn=512):
    m = a.shape[0]
    d = wpa.shape[1]
    rank = wgu.shape[0]
    tm, tn = min(tm, m), min(tn, d)
    assert m % tm == 0 and d % tn == 0 and hd_col % rank == 0
    nj = d // tn
    hb = hd_col // rank

    def act(w):
        return pl.BlockSpec((tm, w), lambda i, j: (i, 0))

    def wt(rows, off):
        return pl.BlockSpec((rows, tn), lambda i, j: (0, off * nj + j))

    return pl.pallas_call(
        _merge_kernel,
        grid=(m // tm, nj),
        in_specs=[act(a.shape[1]), act(b.shape[1]), act(n.shape[1]),
                  pl.BlockSpec((tm, rank), lambda i, j: (i, hb)),
                  wt(wpa.shape[0], 0), wt(wpb.shape[0], 0), wt(wpc.shape[0], 0),
                  wt(rank, 0), wt(rank, 1), wt(rank, 2), wt(1, 0), wt(1, 1), wt(1, 2)],
        out_specs=pl.BlockSpec((tm, tn), lambda i, j: (i, j)),
        out_shape=jax.ShapeDtypeStruct((m, d), BF16),
        compiler_params=_params("arbitrary", "arbitrary"),
        name="branch_merge",
    )(a, b, n, hd, wpa, wpb, wpc, wgu, wgu, wgu, bg, bg, bg)


def _rope_pattern(n_tokens):
    t = np.arange(n_tokens)
    row = jnp.asarray(t // GRID_W, F32)
    col = jnp.asarray(t % GRID_W, F32)
    axis_dim = MLA_ROPE // 2
    inv = ROPE_THETA ** (-jnp.arange(0, axis_dim, 2, dtype=F32) / axis_dim)
    ang_r, ang_c = row[:, None] * inv, col[:, None] * inv
    ang = jnp.concatenate([ang_r, ang_r, ang_c, ang_c], axis=1)
    return jnp.cos(ang), jnp.sin(ang)


def _rotate_half_matrix(width, lanes):
    p = np.zeros((width, width), np.float32)
    q = MLA_ROPE // 4
    for i in lanes:
        if (i % (2 * q)) < q:
            p[i + q, i] = -1.0
        else:
            p[i - q, i] = 1.0
    return p


def _group_matrix(width, group):
    idx = np.arange(width) // group
    return (idx[:, None] == idx[None, :]).astype(np.float32)


def _na_bias_table(rpb):
    heads = rpb.shape[0]
    edge = GRID_W - NA_KW
    rp = jnp.pad(rpb.astype(F32), ((0, 0), (NA_WROWS - 1, NA_WROWS - 1), (edge, edge)))
    by_col = jnp.stack([rp[:, :, GRID_W - 1 - q:2 * GRID_W - 1 - q] for q in range(GRID_W)], axis=2)
    qc = np.arange(GRID_W)[:, None]
    kcol = np.arange(GRID_W)[None, :]
    start = np.clip(qc - NA_KW // 2, 0, edge)
    valid_col = (kcol >= start) & (kcol < start + NA_KW)
    jr = np.arange(NA_WROWS)
    half = NA_KH // 2
    cases = [[(i, 0) for i in range(NA_QROWS)],
             [(half + i, i) for i in range(NA_QROWS)],
             [(NA_KH + i, NA_WROWS - NA_KH) for i in range(NA_QROWS)]]
    tables = []
    for case in cases:
        per_row = []
        for dq, dr0 in case:
            lo = NA_WROWS - 1 + NA_KH - 1 - dq
            valid = ((jr >= dr0) & (jr < dr0 + NA_KH))[:, None, None] & valid_col[None]
            tile = jnp.where(valid[None], by_col[:, lo:lo + NA_WROWS] * LOG2E, MASKED)
            per_row.append(tile.transpose(0, 2, 1, 3).reshape(heads, GRID_W, NA_WROWS * GRID_W))
        tables.append(jnp.concatenate(per_row, axis=1))
    return jnp.stack(tables, axis=1)


def kernel(x, c, ctx, c_ctx, w_ada_down, w_ada_up, b_ada, norm_attn_g, norm_mlp_g, w_in, diff_lambda, diff_q_norm, diff_k_norm, diff_subln, mla_q_a_norm, mla_kv_a_norm, w_mla_uq, w_mla_ukv, mla_q_norm, mla_k_norm, na_q_norm, na_k_norm, na_rpb, w_proj_a, w_proj_b, w_proj_c, w_gate_down, w_gate_up, b_gate, w_out, w_mlp_up, w_mlp_down):
    batch, seq, d = x.shape
    assert batch == 1
    depth = w_in.shape[0]
    hd_n = w_proj_a.shape[1] // DIFF_VD
    hm_n = w_proj_b.shape[1] // MLA_V
    hn_n = w_proj_c.shape[1] // NA_DH
    q_rank = mla_q_a_norm.shape[1]
    kv_rank = mla_kv_a_norm.shape[1]
    g_rank = w_gate_down.shape[2]
    wd, wm, wn = hd_n * DIFF_VD, hm_n * MLA_QK_PAD, hn_n * NA_DH
    col_dq, col_dk, col_dv, col_cq = 0, wd, 2 * wd, 3 * wd
    col_nq = col_cq + q_rank
    col_nk, col_nv = col_nq + wn, col_nq + 2 * wn
    col_gd, col_kv = 0, g_rank + LANES

    cos64, sin64 = _rope_pattern(seq)
    ones64 = jnp.ones((seq, MLA_ROPE), F32)
    zeros64 = jnp.zeros((seq, MLA_ROPE), F32)
    cos_d, sin_d = jnp.tile(cos64, (1, 2)), jnp.tile(sin64, (1, 2))
    cos_m = jnp.concatenate([ones64, ones64, cos64, ones64], axis=1)
    sin_m = jnp.concatenate([zeros64, zeros64, sin64, zeros64], axis=1)
    pm_d = jnp.asarray(_rotate_half_matrix(LANES, range(LANES)), BF16)
    pm_m = jnp.asarray(_rotate_half_matrix(MLA_QK_PAD, range(MLA_NOPE, MLA_QK)), BF16)
    gm_d = jnp.asarray(_group_matrix(LANES, DIFF_DH), BF16)
    gm_m = jnp.asarray(_group_matrix(MLA_QK_PAD, MLA_QK_PAD), BF16)
    gm_n = jnp.asarray(_group_matrix(LANES, NA_DH), BF16)
    rot_d = np.abs(_rotate_half_matrix(LANES, range(LANES))).argmax(axis=0)

    cond = jnp.zeros((16, d), F32).at[0].set(c[0]).at[1].set(c_ctx)
    mod = _ada(cond, w_ada_down.astype(BF16), w_ada_up.astype(BF16), b_ada)

    def gain_row(g, reps, scale=1.0):
        return jnp.tile(g.astype(F32) * scale, reps)[None, :]

    def pad_heads(g):
        return jnp.concatenate([g.astype(F32), jnp.zeros((MLA_QK_PAD - MLA_QK,), F32)])

    xl, xc = x[0], ctx[0]
    for l in range(depth):
        need_ctx = l < depth - 1
        lam_init = 0.8 - 0.6 * math.exp(-0.3 * l)

        wi = w_in[l]
        sp = np.cumsum([0, wd, wd, wd, q_rank, kv_rank, MLA_ROPE, wn, wn, wn])
        part = [wi[:, sp[i]:sp[i + 1]] for i in range(9)]
        head_major = lambda w: w.reshape(d, 2, hd_n, DIFF_DH).transpose(0, 2, 1, 3).reshape(d, wd)
        w1a = jnp.concatenate([head_major(part[0]), head_major(part[1]), part[2], part[3],
                               part[6], part[7], part[8]], axis=1).astype(BF16)
        w1b = jnp.concatenate([w_gate_down[l], jnp.zeros((d, LANES), F32), part[4], part[5],
                               jnp.zeros((d, MLA_ROPE), F32)], axis=1).astype(BF16)
        uq = w_mla_uq[l].reshape(q_rank, hm_n, MLA_QK)
        w_uq = jnp.concatenate([uq, jnp.zeros((q_rank, hm_n, MLA_QK_PAD - MLA_QK), F32)],
                               axis=2).reshape(q_rank, wm).astype(BF16)
        ukv = w_mla_ukv[l].reshape(kv_rank, hm_n, MLA_NOPE + MLA_V)
        k_rows = jnp.concatenate([ukv[:, :, :MLA_NOPE], jnp.zeros((kv_rank, hm_n, MLA_QK_PAD - MLA_NOPE), F32)], axis=2)
        pe_rows = jnp.concatenate([jnp.zeros((MLA_ROPE, hm_n, MLA_NOPE), F32),
                                   jnp.broadcast_to(jnp.eye(MLA_ROPE, dtype=F32)[:, None, :], (MLA_ROPE, hm_n, MLA_ROPE)),
                                   jnp.zeros((MLA_ROPE, hm_n, MLA_QK_PAD - MLA_QK), F32)], axis=2)
        w_kv = jnp.concatenate([
            jnp.concatenate([k_rows.reshape(kv_rank, wm), ukv[:, :, MLA_NOPE:].reshape(kv_rank, hm_n * MLA_V)], axis=1),
            jnp.concatenate([pe_rows.reshape(MLA_ROPE, wm), jnp.zeros((MLA_ROPE, hm_n * MLA_V), F32)], axis=1),
            jnp.zeros((MLA_ROPE, wm + hm_n * MLA_V), F32)], axis=0).astype(BF16)
        kv_gain = jnp.concatenate([mla_kv_a_norm[l].astype(F32), jnp.ones((2 * MLA_ROPE,), F32)])[None, :]
        wpa, wpb, wpc = w_proj_a[l].astype(BF16), w_proj_b[l].astype(BF16), w_proj_c[l].astype(BF16)
        wgu, bg = w_gate_up[l].astype(BF16), b_gate[l][None, :]
        wo, wup, wdn = w_out[l].astype(BF16), w_mlp_up[l].astype(BF16), w_mlp_down[l].astype(BF16)

        dq_gain = gain_row(diff_q_norm[l], 2, DIFF_DH ** -0.5 * LOG2E)
        dk_gain = gain_row(diff_k_norm[l], 2)
        mq_gain = pad_heads(mla_q_norm[l])[None, :] * (MLA_QK ** -0.5 * LOG2E)
        mk_gain = pad_heads(mla_k_norm[l])[None, :]
        nq_gain = gain_row(na_q_norm[l], 1, NA_DH ** -0.5 * LOG2E)
        nk_gain = gain_row(na_k_norm[l], 1)
        rot_m = np.arange(MLA_QK_PAD)
        rot_m[MLA_NOPE:MLA_QK] = MLA_NOPE + rot_d[:MLA_ROPE]
        na_bias = _na_bias_table(na_rpb[l])
        lam_p = diff_lambda[l].astype(F32)
        sub_g = diff_subln[l].astype(F32)[None, :]

        def front(xs, mrow, rotate):
            m6 = [mod[l, mrow, i * d:(i + 1) * d][None, :] for i in range(N_MOD)]
            h = _modulate(xs, norm_attn_g[l][None, :], m6[0], m6[1])
            pa = _matmul(h, w1a, name="in_proj_a")
            pb = _matmul(h, w1b, tn=w1b.shape[1], name="in_proj_b")

            def rope_args(gain, rot_idx, pmat, cos, sin):
                return (gain[:, rot_idx], pmat, cos, sin) if rotate else None

            tile_d = lambda g: jnp.tile(g, (1, hd_n))
            dq = _headnorm(pa, col_dq, wd, wb=LANES, gain=tile_d(dq_gain), gmat=gm_d, inv_cnt=1.0 / DIFF_DH,
                           rope=rope_args(tile_d(dq_gain), np.tile(rot_d, hd_n) + np.repeat(np.arange(hd_n) * LANES, LANES), pm_d, cos_d, sin_d),
                           name="diff_q_prep")
            dk = _headnorm(pa, col_dk, wd, wb=LANES, gain=tile_d(dk_gain), gmat=gm_d, inv_cnt=1.0 / DIFF_DH,
                           rope=rope_args(tile_d(dk_gain), np.tile(rot_d, hd_n) + np.repeat(np.arange(hd_n) * LANES, LANES), pm_d, cos_d, sin_d),
                           name="diff_k_prep")
            q_raw = _matmul(pa, w_uq, a_col=col_cq, norm_gain=mla_q_a_norm[l].astype(F32)[None, :], n_norm=q_rank,
                            name="mla_q_up")
            kv_raw = _matmul(pb, w_kv, a_col=col_kv, norm_gain=kv_gain, n_norm=kv_rank, name="mla_kv_up")
            tile_m = lambda g: jnp.tile(g, (1, hm_n))
            rot_m_all = np.tile(rot_m, hm_n) + np.repeat(np.arange(hm_n) * MLA_QK_PAD, MLA_QK_PAD)
            mq = _headnorm(q_raw, 0, wm, wb=MLA_QK_PAD, gain=tile_m(mq_gain), gmat=gm_m, inv_cnt=1.0 / MLA_QK,
                           rope=rope_args(tile_m(mq_gain), rot_m_all, pm_m, cos_m, sin_m), tm=512, name="mla_q_prep")
            mk = _headnorm(kv_raw, 0, wm, wb=MLA_QK_PAD, gain=tile_m(mk_gain), gmat=gm_m, inv_cnt=1.0 / MLA_QK,
                           rope=rope_args(tile_m(mk_gain), rot_m_all, pm_m, cos_m, sin_m), tm=512, name="mla_k_prep")
            tile_n = lambda g: jnp.tile(g, (1, hn_n))
            nq = _headnorm(pa, col_nq, wn, wb=LANES, gain=tile_n(nq_gain), gmat=gm_n, inv_cnt=1.0 / NA_DH, name="na_q_prep")
            nk = _headnorm(pa, col_nk, wn, wb=LANES, gain=tile_n(nk_gain), gmat=gm_n, inv_cnt=1.0 / NA_DH, name="na_k_prep")
            return dict(m6=m6, h=h, pa=pa, pb=pb, dq=dq, dk=dk, mq=mq, mk=mk, kv_raw=kv_raw, nq=nq, nk=nk)

        def back(xs, f, o_a, o_b, o_c):
            m6 = f["m6"]
            merged = _merge(o_a, o_b, o_c, f["pb"], col_gd, wpa, wpb, wpc, wgu, bg)
            xs = _matmul(merged, wo, out_dtype=F32, epilogue="resid", resid=xs, gate=m6[2], tn=512, name="out_proj")
            h2 = _modulate(xs, norm_mlp_g[l][None, :], m6[3], m6[4])
            hid = _matmul(h2, wup, epilogue="relu2", name="mlp_up")
            return _matmul(hid, wdn, out_dtype=F32, epilogue="resid", resid=xs, gate=m6[5], tk=2048, name="mlp_down")

        fc = front(xc, 1, False)
        fl = front(xl, 0, True)
        dv_c, dv_l = fc["pa"][:, col_dv:col_dv + wd], fl["pa"][:, col_dv:col_dv + wd]
        mv_c, mv_l = fc["kv_raw"][:, wm:], fl["kv_raw"][:, wm:]
        dk_all = jnp.concatenate([fc["dk"], fl["dk"]], axis=0)
        dv_all = _with_ones(jnp.concatenate([dv_c, dv_l], axis=0), hd_n)
        mk_all = jnp.concatenate([fc["mk"], fl["mk"]], axis=0)
        mv_all = _with_ones(jnp.concatenate([mv_c, mv_l], axis=0), hm_n)
        diff_args = (lam_p, sub_g, lam_init)

        a_l = _flash(fl["dq"], dk_all, dv_all, heads=hd_n, dqk=DIFF_VD, diff=diff_args, keys_transposed=True,
                     tq=256, row_chunk=2 * ROW_CHUNK, name="diff_attn")
        b_l = _flash(fl["mq"], mk_all, mv_all, heads=hm_n, dqk=MLA_QK_PAD, keys_transposed=True,
                     row_chunk=2 * ROW_CHUNK, name="mla_attn")
        n_l = _na_attention(fl["nq"], fl["nk"], fl["pa"], col_nv, fc["nk"], fc["pa"], col_nv, na_bias, heads=hn_n)
        xl = back(xl, fl, a_l, b_l, n_l)
        if need_ctx:
            nv_c = fc["pa"][:, col_nv:col_nv + wn]
            a_c = _flash(fc["dq"], fc["dk"], _with_ones(dv_c, hd_n), heads=hd_n, dqk=DIFF_VD, diff=diff_args,
                         name="diff_attn_ctx")
            b_c = _flash(fc["mq"], fc["mk"], _with_ones(mv_c, hm_n), heads=hm_n, dqk=MLA_QK_PAD, name="mla_attn_ctx")
            n_c = _flash(fc["nq"], fc["nk"], _with_ones(nv_c, hn_n), heads=hn_n, dqk=NA_DH, name="na_attn_ctx")
            xc = back(xc, fc, a_c, b_c, n_c)
    return xl[None]
```

```python
import functools
import math

import numpy as np
import jax
import jax.numpy as jnp
from jax import lax
from jax.experimental import pallas as pl
from jax.experimental.pallas import tpu as pltpu

F32 = jnp.float32
BF16 = jnp.bfloat16

NORM_EPS = 1e-6
ROPE_THETA = 10000.0
GRID_W = 64
DIFF_DH = 64
DIFF_VD = 2 * DIFF_DH
MLA_NOPE = 128
MLA_ROPE = 64
MLA_QK = MLA_NOPE + MLA_ROPE
MLA_QK_PAD = 256
MLA_V = 128
NA_DH = 128
NA_KH = 8
NA_KW = 16
NA_QROWS = 4
NA_WROWS = 12
N_MOD = 6
LOG2E = 1.4426950408889634
MASKED = -1e30
LANES = 128
GROUP_BLOCKS = 4
ROW_CHUNK = 32
VMEM_LIMIT_BYTES = 56 * 1024 * 1024


def _params(*semantics):
    return pltpu.CompilerParams(dimension_semantics=semantics, vmem_limit_bytes=VMEM_LIMIT_BYTES)


def _dot(a, b):
    return jnp.dot(a, b, preferred_element_type=F32)


def _dot_nt(a, b):
    return lax.dot_general(a, b, (((1,), (1,)), ((), ())), preferred_element_type=F32)


def _ada_kernel(cond_ref, wd_ref, wu_ref, b_ref, o_ref):
    cnd = cond_ref[...]
    act = cnd / (1.0 + jnp.exp(-cnd))
    low = _dot(act.astype(BF16), wd_ref[...])
    o_ref[0] = _dot(low.astype(BF16), wu_ref[0]) + b_ref[0]


def _ada(cond, w_down, w_up, b):
    rows, d = cond.shape
    depth, rank, n = w_up.shape
    tn = min(n, 4096)
    return pl.pallas_call(
        _ada_kernel,
        grid=(depth, n // tn),
        in_specs=[pl.BlockSpec((rows, d), lambda l, j: (0, 0)),
                  pl.BlockSpec((d, rank), lambda l, j: (0, 0)),
                  pl.BlockSpec((1, rank, tn), lambda l, j: (l, 0, j)),
                  pl.BlockSpec((1, 1, tn), lambda l, j: (l, 0, j))],
        out_specs=pl.BlockSpec((1, rows, tn), lambda l, j: (l, 0, j)),
        out_shape=jax.ShapeDtypeStruct((depth, rows, n), F32),
        compiler_params=_params("arbitrary", "arbitrary"),
        name="ada_mod",
    )(cond, w_down, w_up, b.reshape(depth, 1, n))


def _modulate_kernel(x_ref, g_ref, sh_ref, sc_ref, o_ref):
    x = x_ref[...]
    r = lax.rsqrt(jnp.mean(x * x, axis=-1, keepdims=True) + NORM_EPS)
    y = (x * r) * g_ref[...]
    o_ref[...] = (y * (1.0 + sc_ref[...]) + sh_ref[...]).astype(o_ref.dtype)


def _modulate(x, g, shift, scale, tm=512):
    m, d = x.shape
    tm = min(tm, m)
    vec = pl.BlockSpec((1, d), lambda i: (0, 0))
    return pl.pallas_call(
        _modulate_kernel,
        grid=(m // tm,),
        in_specs=[pl.BlockSpec((tm, d), lambda i: (i, 0)), vec, vec, vec],
        out_specs=pl.BlockSpec((tm, d), lambda i: (i, 0)),
        out_shape=jax.ShapeDtypeStruct((m, d), BF16),
        compiler_params=_params("arbitrary"),
        name="modulate",
    )(x, g, shift, scale)


def _mm_kernel(*refs, nk, n_norm, epilogue):
    it = iter(refs)
    a_ref, w_ref = next(it), next(it)
    g_ref = next(it) if n_norm else None
    x_ref, gate_ref = (next(it), next(it)) if epilogue == "resid" else (None, None)
    o_ref = next(it)
    acc_ref = next(it) if nk > 1 else None

    a = a_ref[...]
    if n_norm:
        af = a.astype(F32)
        if n_norm < af.shape[1]:
            normed = lax.broadcasted_iota(jnp.int32, af.shape, 1) < n_norm
            ssq = jnp.sum(jnp.where(normed, af * af, 0.0), axis=-1, keepdims=True)
            r = lax.rsqrt(ssq * (1.0 / n_norm) + NORM_EPS)
            a = jnp.where(normed, (af * r) * g_ref[...], af).astype(BF16)
        else:
            r = lax.rsqrt(jnp.mean(af * af, axis=-1, keepdims=True) + NORM_EPS)
            a = ((af * r) * g_ref[...]).astype(BF16)
    def finish(acc):
        if epilogue == "relu2":
            acc = jnp.maximum(acc, 0.0)
            acc = acc * acc
        elif epilogue == "resid":
            acc = x_ref[...] + gate_ref[...] * acc
        o_ref[...] = acc.astype(o_ref.dtype)

    if nk == 1:
        finish(_dot(a, w_ref[...]))
    else:
        k = pl.program_id(2)

        @pl.when(k == 0)
        def _():
            acc_ref[...] = jnp.zeros(acc_ref.shape, F32)

        acc_ref[...] += _dot(a, w_ref[...])

        @pl.when(k == nk - 1)
        def _():
            finish(acc_ref[...])


def _matmul(a, w, *, a_col=0, out_dtype=BF16, norm_gain=None, n_norm=0, epilogue=None,
            resid=None, gate=None, tm=1024, tn=1024, tk=None, name="matmul"):
    m = a.shape[0]
    k_dim, n = w.shape
    tm, tn, tk = min(tm, m), min(tn, n), min(tk or k_dim, k_dim)
    assert m % tm == 0 and n % tn == 0 and k_dim % tk == 0 and a_col % tk == 0, (m, n, k_dim, tm, tn, tk, a_col)
    nk = k_dim // tk
    a_off = a_col // tk
    assert not (n_norm and nk > 1)
    in_specs = [pl.BlockSpec((tm, tk), lambda i, j, k: (i, a_off + k)),
                pl.BlockSpec((tk, tn), lambda i, j, k: (k, j))]
    args = [a, w]
    if n_norm:
        in_specs.append(pl.BlockSpec((1, tk), lambda i, j, k: (0, 0)))
        args.append(norm_gain)
    aliases = {}
    if epilogue == "resid":
        aliases = {len(args): 0}
        in_specs += [pl.BlockSpec((tm, tn), lambda i, j, k: (i, j)),
                     pl.BlockSpec((1, tn), lambda i, j, k: (0, j))]
        args += [resid, gate]
    return pl.pallas_call(
        functools.partial(_mm_kernel, nk=nk, n_norm=n_norm, epilogue=epilogue),
        grid=(m // tm, n // tn, nk),
        in_specs=in_specs,
        out_specs=pl.BlockSpec((tm, tn), lambda i, j, k: (i, j)),
        out_shape=jax.ShapeDtypeStruct((m, n), out_dtype),
        scratch_shapes=[pltpu.VMEM((tm, tn), F32)] if nk > 1 else [],
        input_output_aliases=aliases,
        compiler_params=_params("arbitrary", "arbitrary", "arbitrary"),
        name=name,
    )(*args)


def _headnorm_kernel(*refs, wb, nchunk, inv_cnt, rope):
    if rope:
        x_ref, g_ref, gm_ref, gr_ref, pm_ref, cos_ref, sin_ref, o_ref = refs
    else:
        x_ref, g_ref, gm_ref, o_ref = refs
    for c in range(nchunk):
        cols = slice(c * wb, (c + 1) * wb)
        xb = x_ref[:, cols]
        xf = xb.astype(F32)
        sq = xf * xf
        hi = sq.astype(BF16)
        lo = (sq - hi.astype(F32)).astype(BF16)
        ssq = _dot(hi, gm_ref[...]) + _dot(lo, gm_ref[...])
        r = lax.rsqrt(ssq * inv_cnt + NORM_EPS)
        y = (xf * r) * g_ref[:, cols]
        if rope:
            yr = (_dot(xb, pm_ref[...]) * r) * gr_ref[:, cols]
            y = y * cos_ref[...] + yr * sin_ref[...]
        o_ref[:, cols] = y.astype(o_ref.dtype)


def _headnorm(x, col, width, *, wb, gain, gmat, inv_cnt, rope=None, tm=1024, name="headnorm"):
    m = x.shape[0]
    tm = min(tm, m)
    assert m % tm == 0 and col % width == 0 and width % wb == 0
    cb = col // width
    row = pl.BlockSpec((1, width), lambda i: (0, 0))
    mat = pl.BlockSpec((wb, wb), lambda i: (0, 0))
    in_specs = [pl.BlockSpec((tm, width), lambda i: (i, cb)), row, mat]
    args = [x, gain, gmat]
    if rope is not None:
        gain_rot, pmat, cos, sin = rope
        tab = pl.BlockSpec((tm, wb), lambda i: (i, 0))
        in_specs += [row, mat, tab, tab]
        args += [gain_rot, pmat, cos, sin]
    return pl.pallas_call(
        functools.partial(_headnorm_kernel, wb=wb, nchunk=width // wb, inv_cnt=inv_cnt, rope=rope is not None),
        grid=(m // tm,),
        in_specs=in_specs,
        out_specs=pl.BlockSpec((tm, width), lambda i: (i, 0)),
        out_shape=jax.ShapeDtypeStruct((m, width), BF16),
        compiler_params=_params("arbitrary"),
        name=name,
    )(*args)


def _flash_kernel(*refs, diff, keys_transposed, tq, tk, n, qb, gb, row_chunk, lam_init):
    if diff:
        q_ref, k_ref, v_ref, lp_ref, sg_ref, o_ref, qs_ref, s0, s1, p0, p1, a0, a1, m_ref, acc_ref = refs
    else:
        q_ref, k_ref, v_ref, o_ref, s0, s1, p0, p1, a0, a1, m_ref, acc_ref = refs
    s_slots, p_slots, a_slots = (s0, s1), (p0, p1), (a0, a1)
    rows = m_ref.shape[1]
    total = gb * n
    pairs = max(0, (n - 2) // 2)

    def key_rows(j):
        off = j * tk
        return pl.ds(off if isinstance(j, int) else pl.multiple_of(off, 2 * LANES), tk)

    def group(g, carry):
        def q_rows(blk):
            first = (g * gb + blk) * tq
            return pl.ds(first if isinstance(first, int) else pl.multiple_of(first, tq), tq)

        def scores(blk, j, slot):
            q = qs_ref[blk] if diff else q_ref[q_rows(blk), :]
            if keys_transposed:
                s_slots[slot][...] = _dot(q, k_ref[0, j])
            else:
                s_slots[slot][...] = _dot_nt(q, k_ref[key_rows(j), :])

        def softmax(blk, slot):
            s_ref, p_ref, a_ref = s_slots[slot], p_slots[slot], a_slots[slot]
            for r in range(0, rows, row_chunk):
                rs = slice(r, r + row_chunk)
                m_prev = m_ref[blk, rs, :]
                m_new = jnp.maximum(m_prev, jnp.max(s_ref[rs, :], axis=1, keepdims=True))
                a_ref[rs, :] = jnp.exp2(m_prev - m_new)
                m_ref[blk, rs, :] = m_new
                p_ref[rs, :] = jnp.exp2(s_ref[rs, :] - m_new[:, :1]).astype(BF16)

        def values(blk, j, slot):
            alpha = a_slots[slot][...]
            acc_ref[blk] = (jnp.concatenate([alpha, alpha], axis=1) * acc_ref[blk]
                            + _dot(p_slots[slot][...], v_ref[key_rows(j), :]))

        def finalize(blk):
            acc = acc_ref[blk]
            o = acc[:, :LANES] / acc[:, LANES:]
            if diff:
                lp = lp_ref[...]
                lam = (jnp.exp(jnp.sum(lp[0:1] * lp[1:2], axis=1, keepdims=True))
                       - jnp.exp(jnp.sum(lp[2:3] * lp[3:4], axis=1, keepdims=True)) + lam_init)
                d = o[0:tq] - lam * o[tq:2 * tq]
                r = lax.rsqrt(jnp.mean(d * d, axis=-1, keepdims=True) + NORM_EPS)
                o = ((d * r) * sg_ref[...]) * (1.0 - lam_init)
            o_ref[q_rows(blk), :] = o.astype(o_ref.dtype)

        def run(qk, sm, pv):
            if qk:
                scores(*qk)
            if sm:
                softmax(*sm)
            if pv:
                values(*pv)
                if isinstance(pv[1], int) and pv[1] == n - 1:
                    finalize(pv[0])

        for blk in range(gb):
            m_ref[blk] = jnp.full((rows, LANES), MASKED, F32)
            acc_ref[blk] = jnp.zeros((rows, 2 * LANES), F32)
            if diff:
                q = q_ref[q_rows(blk), :].astype(F32)
                first = lax.broadcasted_iota(jnp.int32, q.shape, 1) < DIFF_DH
                qs_ref[blk, 0:tq, :] = jnp.where(first, q, 0.0).astype(BF16)
                qs_ref[blk, tq:2 * tq, :] = jnp.where(first, 0.0, q).astype(BF16)

        t = 0
        while t < total + 2:
            blk, j = divmod(t, n)
            if t < total and j == 2 and pairs:
                def pair(i, c, blk=blk, even=t % 2, odd=(t + 1) % 2):
                    jj = 2 + 2 * i
                    run((blk, jj, even), (blk, odd), (blk, jj - 2, even))
                    run((blk, jj + 1, odd), (blk, even), (blk, jj - 1, odd))
                    return c

                lax.fori_loop(0, pairs, pair, 0)
                t += 2 * pairs
            else:
                run((blk, j, t % 2) if t < total else None,
                    ((t - 1) // n, (t - 1) % 2) if 1 <= t <= total else None,
                    ((t - 2) // n, (t - 2) % n, t % 2) if t >= 2 else None)
                t += 1
        return carry

    if qb == gb:
        group(0, 0)
    else:
        lax.fori_loop(0, qb // gb, group, 0)


def _with_ones(v, heads):
    v3 = v.reshape(v.shape[0], heads, LANES)
    return jnp.concatenate([v3, jnp.ones_like(v3)], axis=2).reshape(v.shape[0], 2 * heads * LANES)


def _key_chunks(k, heads, tk):
    skv, width = k.shape
    return k.reshape(skv // tk, tk, heads, width // heads).transpose(2, 0, 3, 1)


def _flash(q, k, v_ones, *, heads, dqk, diff=None, keys_transposed=False, tq=512, tk=1280, tokens=2048,
           row_chunk=ROW_CHUNK, name="flash"):
    m = q.shape[0]
    skv = k.shape[0]
    tq, tk = min(tq, m), min(tk, skv)
    tokens = max(tq, min(tokens, m))
    assert m % tokens == 0 and tokens % tq == 0 and skv % tk == 0 and k.shape[1] == heads * dqk
    assert tk % (2 * LANES) == 0
    n = skv // tk
    rows = 2 * tq if diff else tq
    assert rows % row_chunk == 0
    qb = tokens // tq
    gb = GROUP_BLOCKS if qb % GROUP_BLOCKS == 0 else 1
    if keys_transposed:
        k_arg, k_spec = _key_chunks(k, heads, tk), pl.BlockSpec((1, n, dqk, tk), lambda h, i: (h, 0, 0, 0))
    else:
        k_arg, k_spec = k, pl.BlockSpec((skv, dqk), lambda h, i: (0, h))
    in_specs = [pl.BlockSpec((tokens, dqk), lambda h, i: (i, h)), k_spec,
                pl.BlockSpec((skv, 2 * LANES), lambda h, i: (0, h))]
    args = [q, k_arg, v_ones]
    scratch = []
    lam_init = 0.0
    if diff:
        lam_p, sub_g, lam_init = diff
        in_specs += [pl.BlockSpec(lam_p.shape, lambda h, i: (0, 0)),
                     pl.BlockSpec((1, LANES), lambda h, i: (0, 0))]
        args += [lam_p, sub_g]
        scratch.append(pltpu.VMEM((gb, rows, dqk), BF16))
    scratch += [pltpu.VMEM((rows, tk), F32), pltpu.VMEM((rows, tk), F32),
                pltpu.VMEM((rows, tk), BF16), pltpu.VMEM((rows, tk), BF16),
                pltpu.VMEM((rows, LANES), F32), pltpu.VMEM((rows, LANES), F32),
                pltpu.VMEM((gb, rows, LANES), F32), pltpu.VMEM((gb, rows, 2 * LANES), F32)]
    return pl.pallas_call(
        functools.partial(_flash_kernel, diff=bool(diff), keys_transposed=keys_transposed, tq=tq, tk=tk, n=n,
                          qb=qb, gb=gb, row_chunk=row_chunk, lam_init=lam_init),
        grid=(heads, m // tokens),
        in_specs=in_specs,
        out_specs=pl.BlockSpec((tokens, LANES), lambda h, i: (i, h)),
        out_shape=jax.ShapeDtypeStruct((m, heads * LANES), BF16),
        scratch_shapes=scratch,
        compiler_params=_params("arbitrary", "arbitrary"),
        name=name,
    )(*args)


def _na_kernel(q_ref, k_ref, v_ref, kc_ref, vc_ref, b_ref, o_ref, *, nb, rows):
    step = pl.program_id(1)
    kc = kc_ref[...]
    vc = vc_ref[...]
    tq, win = NA_QROWS * GRID_W, NA_WROWS * GRID_W
    last = rows // NA_QROWS - 1

    def one_block(bb, carry):
        b = step * nb + bb
        first_row = jnp.clip(b * NA_QROWS - NA_KH // 2, 0, rows - NA_WROWS)
        case = jnp.where(b == 0, 0, jnp.where(b == last, 2, 1))
        q_rows = pl.ds(pl.multiple_of(bb * tq, tq), tq)
        k_rows = pl.ds(pl.multiple_of(first_row * GRID_W, tq), win)
        q = q_ref[q_rows, :]
        s_win = _dot_nt(q, k_ref[k_rows, :]) + b_ref[0, case]
        s_ctx = _dot_nt(q, kc)
        m = jnp.maximum(jnp.max(s_win, axis=1, keepdims=True), jnp.max(s_ctx, axis=1, keepdims=True))
        p_win = jnp.exp2(s_win - m)
        p_ctx = jnp.exp2(s_ctx - m)
        l = jnp.sum(p_win, axis=1, keepdims=True) + jnp.sum(p_ctx, axis=1, keepdims=True)
        o = _dot(p_win.astype(BF16), v_ref[k_rows, :]) + _dot(p_ctx.astype(BF16), vc)
        o_ref[q_rows, :] = (o / l).astype(o_ref.dtype)
        return carry

    lax.fori_loop(0, nb, one_block, 0)


def _na_attention(q, k, v, v_col, kc, vc, vc_col, bias, *, heads, nb=4):
    s = q.shape[0]
    rows = s // GRID_W
    nb = min(nb, rows // NA_QROWS)
    assert s % GRID_W == 0 and rows >= NA_WROWS and rows % (NA_QROWS * nb) == 0
    nctx = kc.shape[0]
    vb, vcb = v_col // NA_DH, vc_col // NA_DH
    tq = nb * NA_QROWS * GRID_W
    return pl.pallas_call(
        functools.partial(_na_kernel, nb=nb, rows=rows),
        grid=(heads, s // tq),
        in_specs=[pl.BlockSpec((tq, NA_DH), lambda h, i: (i, h)),
                  pl.BlockSpec((s, NA_DH), lambda h, i: (0, h)),
                  pl.BlockSpec((s, NA_DH), lambda h, i: (0, vb + h)),
                  pl.BlockSpec((nctx, NA_DH), lambda h, i: (0, h)),
                  pl.BlockSpec((nctx, NA_DH), lambda h, i: (0, vcb + h)),
                  pl.BlockSpec((1,) + bias.shape[1:], lambda h, i: (h, 0, 0, 0))],
        out_specs=pl.BlockSpec((tq, NA_DH), lambda h, i: (i, h)),
        out_shape=jax.ShapeDtypeStruct((s, heads * NA_DH), BF16),
        compiler_params=_params("arbitrary", "arbitrary"),
        name="na_attention",
    )(q, k, v, kc, vc, bias)


def _merge_kernel(a_ref, b_ref, n_ref, hd_ref, wpa_ref, wpb_ref, wpc_ref, wga_ref, wgb_ref, wgc_ref,
                  bga_ref, bgb_ref, bgc_ref, o_ref):
    hd = hd_ref[...]

    def gated(o_ref_, wp_ref, wg_ref, bg_ref):
        z = _dot(hd, wg_ref[...]) + bg_ref[...]
        return (1.0 / (1.0 + jnp.exp(-z))) * _dot(o_ref_[...], wp_ref[...])

    out = gated(a_ref, wpa_ref, wga_ref, bga_ref) + gated(b_ref, wpb_ref, wgb_ref, bgb_ref)
    out = out + gated(n_ref, wpc_ref, wgc_ref, bgc_ref)
    o_ref[...] = out.astype(o_ref.dtype)


def _merge(a, b, n, hd, hd_col, wpa, wpb, wpc, wgu, bg, tm=1024, tn=512):
    m = a.shape[0]
    d = wpa.shape[1]
    rank = wgu.shape[0]
    tm, tn = min(tm, m), min(tn, d)
    assert m % tm == 0 and d % tn == 0 and hd_col % rank == 0
    nj = d // tn
    hb = hd_col // rank

    def act(w):
        return pl.BlockSpec((tm, w), lambda i, j: (i, 0))

    def wt(rows, off):
        return pl.BlockSpec((rows, tn), lambda i, j: (0, off * nj + j))

    return pl.pallas_call(
        _merge_kernel,
        grid=(m // tm, nj),
        in_specs=[act(a.shape[1]), act(b.shape[1]), act(n.shape[1]),
                  pl.BlockSpec((tm, rank), lambda i, j: (i, hb)),
                  wt(wpa.shape[0], 0), wt(wpb.shape[0], 0), wt(wpc.shape[0], 0),
                  wt(rank, 0), wt(rank, 1), wt(rank, 2), wt(1, 0), wt(1, 1), wt(1, 2)],
        out_specs=pl.BlockSpec((tm, tn), lambda i, j: (i, j)),
        out_shape=jax.ShapeDtypeStruct((m, d), BF16),
        compiler_params=_params("arbitrary", "arbitrary"),
        name="branch_merge",
    )(a, b, n, hd, wpa, wpb, wpc, wgu, wgu, wgu, bg, bg, bg)


def _rope_pattern(n_tokens):
    t = np.arange(n_tokens)
    row = jnp.asarray(t // GRID_W, F32)
    col = jnp.asarray(t % GRID_W, F32)
    axis_dim = MLA_ROPE // 2
    inv = ROPE_THETA ** (-jnp.arange(0, axis_dim, 2, dtype=F32) / axis_dim)
    ang_r, ang_c = row[:, None] * inv, col[:, None] * inv
    ang = jnp.concatenate([ang_r, ang_r, ang_c, ang_c], axis=1)
    return jnp.cos(ang), jnp.sin(ang)


def _rotate_half_matrix(width, lanes):
    p = np.zeros((width, width), np.float32)
    q = MLA_ROPE // 4
    for i in lanes:
        if (i % (2 * q)) < q:
            p[i + q, i] = -1.0
        else:
            p[i - q, i] = 1.0
    return p


def _group_matrix(width, group):
    idx = np.arange(width) // group
    return (idx[:, None] == idx[None, :]).astype(np.float32)


def _na_bias_table(rpb):
    heads = rpb.shape[0]
    edge = GRID_W - NA_KW
    rp = jnp.pad(rpb.astype(F32), ((0, 0), (NA_WROWS - 1, NA_WROWS - 1), (edge, edge)))
    by_col = jnp.stack([rp[:, :, GRID_W - 1 - q:2 * GRID_W - 1 - q] for q in range(GRID_W)], axis=2)
    qc = np.arange(GRID_W)[:, None]
    kcol = np.arange(GRID_W)[None, :]
    start = np.clip(qc - NA_KW // 2, 0, edge)
    valid_col = (kcol >= start) & (kcol < start + NA_KW)
    jr = np.arange(NA_WROWS)
    half = NA_KH // 2
    cases = [[(i, 0) for i in range(NA_QROWS)],
             [(half + i, i) for i in range(NA_QROWS)],
             [(NA_KH + i, NA_WROWS - NA_KH) for i in range(NA_QROWS)]]
    tables = []
    for case in cases:
        per_row = []
        for dq, dr0 in case:
            lo = NA_WROWS - 1 + NA_KH - 1 - dq
            valid = ((jr >= dr0) & (jr < dr0 + NA_KH))[:, None, None] & valid_col[None]
            tile = jnp.where(valid[None], by_col[:, lo:lo + NA_WROWS] * LOG2E, MASKED)
            per_row.append(tile.transpose(0, 2, 1, 3).reshape(heads, GRID_W, NA_WROWS * GRID_W))
        tables.append(jnp.concatenate(per_row, axis=1))
    return jnp.stack(tables, axis=1)


def kernel(x, c, ctx, c_ctx, w_ada_down, w_ada_up, b_ada, norm_attn_g, norm_mlp_g, w_in, diff_lambda, diff_q_norm, diff_k_norm, diff_subln, mla_q_a_norm, mla_kv_a_norm, w_mla_uq, w_mla_ukv, mla_q_norm, mla_k_norm, na_q_norm, na_k_norm, na_rpb, w_proj_a, w_proj_b, w_proj_c, w_gate_down, w_gate_up, b_gate, w_out, w_mlp_up, w_mlp_down):
    batch, seq, d = x.shape
    assert batch == 1
    depth = w_in.shape[0]
    hd_n = w_proj_a.shape[1] // DIFF_VD
    hm_n = w_proj_b.shape[1] // MLA_V
    hn_n = w_proj_c.shape[1] // NA_DH
    q_rank = mla_q_a_norm.shape[1]
    kv_rank = mla_kv_a_norm.shape[1]
    g_rank = w_gate_down.shape[2]
    wd, wm, wn = hd_n * DIFF_VD, hm_n * MLA_QK_PAD, hn_n * NA_DH
    col_dq, col_dk, col_dv, col_cq = 0, wd, 2 * wd, 3 * wd
    col_nq = col_cq + q_rank
    col_nk, col_nv = col_nq + wn, col_nq + 2 * wn
    col_gd, col_kv = 0, g_rank + LANES

    cos64, sin64 = _rope_pattern(seq)
    ones64 = jnp.ones((seq, MLA_ROPE), F32)
    zeros64 = jnp.zeros((seq, MLA_ROPE), F32)
    cos_d, sin_d = jnp.tile(cos64, (1, 2)), jnp.tile(sin64, (1, 2))
    cos_m = jnp.concatenate([ones64, ones64, cos64, ones64], axis=1)
    sin_m = jnp.concatenate([zeros64, zeros64, sin64, zeros64], axis=1)
    pm_d = jnp.asarray(_rotate_half_matrix(LANES, range(LANES)), BF16)
    pm_m = jnp.asarray(_rotate_half_matrix(MLA_QK_PAD, range(MLA_NOPE, MLA_QK)), BF16)
    gm_d = jnp.asarray(_group_matrix(LANES, DIFF_DH), BF16)
    gm_m = jnp.asarray(_group_matrix(MLA_QK_PAD, MLA_QK_PAD), BF16)
    gm_n = jnp.asarray(_group_matrix(LANES, NA_DH), BF16)
    rot_d = np.abs(_rotate_half_matrix(LANES, range(LANES))).argmax(axis=0)

    cond = jnp.zeros((16, d), F32).at[0].set(c[0]).at[1].set(c_ctx)
    mod = _ada(cond, w_ada_down.astype(BF16), w_ada_up.astype(BF16), b_ada)

    def gain_row(g, reps, scale=1.0):
        return jnp.tile(g.astype(F32) * scale, reps)[None, :]

    def pad_heads(g):
        return jnp.concatenate([g.astype(F32), jnp.zeros((MLA_QK_PAD - MLA_QK,), F32)])

    xl, xc = x[0], ctx[0]
    for l in range(depth):
        need_ctx = l < depth - 1
        lam_init = 0.8 - 0.6 * math.exp(-0.3 * l)

        wi = w_in[l]
        sp = np.cumsum([0, wd, wd, wd, q_rank, kv_rank, MLA_ROPE, wn, wn, wn])
        part = [wi[:, sp[i]:sp[i + 1]] for i in range(9)]
        head_major = lambda w: w.reshape(d, 2, hd_n, DIFF_DH).transpose(0, 2, 1, 3).reshape(d, wd)
        w1a = jnp.concatenate([head_major(part[0]), head_major(part[1]), part[2], part[3],
                               part[6], part[7], part[8]], axis=1).astype(BF16)
        w1b = jnp.concatenate([w_gate_down[l], jnp.zeros((d, LANES), F32), part[4], part[5],
                               jnp.zeros((d, MLA_ROPE), F32)], axis=1).astype(BF16)
        uq = w_mla_uq[l].reshape(q_rank, hm_n, MLA_QK)
        w_uq = jnp.concatenate([uq, jnp.zeros((q_rank, hm_n, MLA_QK_PAD - MLA_QK), F32)],
                               axis=2).reshape(q_rank, wm).astype(BF16)
        ukv = w_mla_ukv[l].reshape(kv_rank, hm_n, MLA_NOPE + MLA_V)
        k_rows = jnp.concatenate([ukv[:, :, :MLA_NOPE], jnp.zeros((kv_rank, hm_n, MLA_QK_PAD - MLA_NOPE), F32)], axis=2)
        pe_rows = jnp.concatenate([jnp.zeros((MLA_ROPE, hm_n, MLA_NOPE), F32),
                                   jnp.broadcast_to(jnp.eye(MLA_ROPE, dtype=F32)[:, None, :], (MLA_ROPE, hm_n, MLA_ROPE)),
                                   jnp.zeros((MLA_ROPE, hm_n, MLA_QK_PAD - MLA_QK), F32)], axis=2)
        w_kv = jnp.concatenate([
            jnp.concatenate([k_rows.reshape(kv_rank, wm), ukv[:, :, MLA_NOPE:].reshape(kv_rank, hm_n * MLA_V)], axis=1),
            jnp.concatenate([pe_rows.reshape(MLA_ROPE, wm), jnp.zeros((MLA_ROPE, hm_n * MLA_V), F32)], axis=1),
            jnp.zeros((MLA_ROPE, wm + hm_n * MLA_V), F32)], axis=0).astype(BF16)
        kv_gain = jnp.concatenate([mla_kv_a_norm[l].astype(F32), jnp.ones((2 * MLA_ROPE,), F32)])[None, :]
        wpa, wpb, wpc = w_proj_a[l].astype(BF16), w_proj_b[l].astype(BF16), w_proj_c[l].astype(BF16)
        wgu, bg = w_gate_up[l].astype(BF16), b_gate[l][None, :]
        wo, wup, wdn = w_out[l].astype(BF16), w_mlp_up[l].astype(BF16), w_mlp_down[l].astype(BF16)

        dq_gain = gain_row(diff_q_norm[l], 2, DIFF_DH ** -0.5 * LOG2E)
        dk_gain = gain_row(diff_k_norm[l], 2)
        mq_gain = pad_heads(mla_q_norm[l])[None, :] * (MLA_QK ** -0.5 * LOG2E)
        mk_gain = pad_heads(mla_k_norm[l])[None, :]
        nq_gain = gain_row(na_q_norm[l], 1, NA_DH ** -0.5 * LOG2E)
        nk_gain = gain_row(na_k_norm[l], 1)
        rot_m = np.arange(MLA_QK_PAD)
        rot_m[MLA_NOPE:MLA_QK] = MLA_NOPE + rot_d[:MLA_ROPE]
        na_bias = _na_bias_table(na_rpb[l])
        lam_p = diff_lambda[l].astype(F32)
        sub_g = diff_subln[l].astype(F32)[None, :]

        def front(xs, mrow, rotate):
            m6 = [mod[l, mrow, i * d:(i + 1) * d][None, :] for i in range(N_MOD)]
            h = _modulate(xs, norm_attn_g[l][None, :], m6[0], m6[1])
            pa = _matmul(h, w1a, name="in_proj_a")
            pb = _matmul(h, w1b, tn=w1b.shape[1], name="in_proj_b")

            def rope_args(gain, rot_idx, pmat, cos, sin):
                return (gain[:, rot_idx], pmat, cos, sin) if rotate else None

            tile_d = lambda g: jnp.tile(g, (1, hd_n))
            dq = _headnorm(pa, col_dq, wd, wb=LANES, gain=tile_d(dq_gain), gmat=gm_d, inv_cnt=1.0 / DIFF_DH,
                           rope=rope_args(tile_d(dq_gain), np.tile(rot_d, hd_n) + np.repeat(np.arange(hd_n) * LANES, LANES), pm_d, cos_d, sin_d),
                           name="diff_q_prep")
            dk = _headnorm(pa, col_dk, wd, wb=LANES, gain=tile_d(dk_gain), gmat=gm_d, inv_cnt=1.0 / DIFF_DH,
                           rope=rope_args(tile_d(dk_gain), np.tile(rot_d, hd_n) + np.repeat(np.arange(hd_n) * LANES, LANES), pm_d, cos_d, sin_d),
                           name="diff_k_prep")
            q_raw = _matmul(pa, w_uq, a_col=col_cq, norm_gain=mla_q_a_norm[l].astype(F32)[None, :], n_norm=q_rank,
                            name="mla_q_up")
            kv_raw = _matmul(pb, w_kv, a_col=col_kv, norm_gain=kv_gain, n_norm=kv_rank, name="mla_kv_up")
            tile_m = lambda g: jnp.tile(g, (1, hm_n))
            rot_m_all = np.tile(rot_m, hm_n) + np.repeat(np.arange(hm_n) * MLA_QK_PAD, MLA_QK_PAD)
            mq = _headnorm(q_raw, 0, wm, wb=MLA_QK_PAD, gain=tile_m(mq_gain), gmat=gm_m, inv_cnt=1.0 / MLA_QK,
                           rope=rope_args(tile_m(mq_gain), rot_m_all, pm_m, cos_m, sin_m), tm=512, name="mla_q_prep")
            mk = _headnorm(kv_raw, 0, wm, wb=MLA_QK_PAD, gain=tile_m(mk_gain), gmat=gm_m, inv_cnt=1.0 / MLA_QK,
                           rope=rope_args(tile_m(mk_gain), rot_m_all, pm_m, cos_m, sin_m), tm=512, name="mla_k_prep")
            tile_n = lambda g: jnp.tile(g, (1, hn_n))
            nq = _headnorm(pa, col_nq, wn, wb=LANES, gain=tile_n(nq_gain), gmat=gm_n, inv_cnt=1.0 / NA_DH, name="na_q_prep")
            nk = _headnorm(pa, col_nk, wn, wb=LANES, gain=tile_n(nk_gain), gmat=gm_n, inv_cnt=1.0 / NA_DH, name="na_k_prep")
            return dict(m6=m6, h=h, pa=pa, pb=pb, dq=dq, dk=dk, mq=mq, mk=mk, kv_raw=kv_raw, nq=nq, nk=nk)

        def back(xs, f, o_a, o_b, o_c):
            m6 = f["m6"]
            merged = _merge(o_a, o_b, o_c, f["pb"], col_gd, wpa, wpb, wpc, wgu, bg)
            xs = _matmul(merged, wo, out_dtype=F32, epilogue="resid", resid=xs, gate=m6[2], tn=512, name="out_proj")
            h2 = _modulate(xs, norm_mlp_g[l][None, :], m6[3], m6[4])
            hid = _matmul(h2, wup, epilogue="relu2", name="mlp_up")
            return _matmul(hid, wdn, out_dtype=F32, epilogue="resid", resid=xs, gate=m6[5], tk=2048, name="mlp_down")

        fc = front(xc, 1, False)
        fl = front(xl, 0, True)
        dv_c, dv_l = fc["pa"][:, col_dv:col_dv + wd], fl["pa"][:, col_dv:col_dv + wd]
        mv_c, mv_l = fc["kv_raw"][:, wm:], fl["kv_raw"][:, wm:]
        dk_all = jnp.concatenate([fc["dk"], fl["dk"]], axis=0)
        dv_all = _with_ones(jnp.concatenate([dv_c, dv_l], axis=0), hd_n)
        mk_all = jnp.concatenate([fc["mk"], fl["mk"]], axis=0)
        mv_all = _with_ones(jnp.concatenate([mv_c, mv_l], axis=0), hm_n)
        diff_args = (lam_p, sub_g, lam_init)

        a_l = _flash(fl["dq"], dk_all, dv_all, heads=hd_n, dqk=DIFF_VD, diff=diff_args, keys_transposed=True,
                     tq=256, row_chunk=2 * ROW_CHUNK, name="diff_attn")
        b_l = _flash(fl["mq"], mk_all, mv_all, heads=hm_n, dqk=MLA_QK_PAD, keys_transposed=True,
                     row_chunk=2 * ROW_CHUNK, name="mla_attn")
        n_l = _na_attention(fl["nq"], fl["nk"], fl["pa"], col_nv, fc["nk"], fc["pa"], col_nv, na_bias, heads=hn_n)
        xl = back(xl, fl, a_l, b_l, n_l)
        if need_ctx:
            nv_c = fc["pa"][:, col_nv:col_nv + wn]
            a_c = _flash(fc["dq"], fc["dk"], _with_ones(dv_c, hd_n), heads=hd_n, dqk=DIFF_VD, diff=diff_args,
                         name="diff_attn_ctx")
            b_c = _flash(fc["mq"], fc["mk"], _with_ones(mv_c, hm_n), heads=hm_n, dqk=MLA_QK_PAD, name="mla_attn_ctx")
            n_c = _flash(fc["nq"], fc["nk"], _with_ones(nv_c, hn_n), heads=hn_n, dqk=NA_DH, name="na_attn_ctx")
            xc = back(xc, fc, a_c, b_c, n_c)
    return xl[None]
```
